```python
import math, functools
import jax, jax.numpy as jnp
from jax import lax
import numpy as np

D_MODEL = 2048
BATCH = 8
SEQ = 2048
DEPTH = 1

MIX_WIDTH = D_MODEL
POOL_WIDTH = MIX_WIDTH // 2
POOL_GROUPS = 4
POOL_GROUP_DIM = POOL_WIDTH // POOL_GROUPS
POOL_WINDOWS = (2, 4, 8, 16)
DN_WIDTH = MIX_WIDTH - POOL_WIDTH
DN_HEAD_DIM = 128
DN_HEADS = DN_WIDTH // DN_HEAD_DIM
CONV_WIDTH = 4
CHUNK = 64
IN_COLS = POOL_WIDTH + 4 * DN_WIDTH + 2 * DN_HEADS
N_GROUPS = 4
EXPERTS_PER_GROUP = 8
N_EXPERTS = N_GROUPS * EXPERTS_PER_GROUP
TOP_K = 2
D_EXPERT = 768
MOE_BLOCK = 128
EPS = 1e-6

kernel_name = "hybrid_pool_deltanet_hmoe_adaln"


def rmsnorm(x, g):
    xf = x.astype(jnp.float32)
    y = xf * lax.rsqrt(jnp.mean(xf * xf, axis=-1, keepdims=True) + EPS)
    return (y * g.astype(jnp.float32)).astype(x.dtype)


def l2norm(x):
    return x * lax.rsqrt(jnp.sum(x * x, axis=-1, keepdims=True) + EPS)


def pool_mixer(u, pool_w, pool_scale):
    B, S, _ = u.shape
    uf = u.reshape(B, S, POOL_GROUPS, POOL_GROUP_DIM).astype(jnp.float32)
    cs = jnp.concatenate([jnp.zeros((B, 1, POOL_GROUPS, POOL_GROUP_DIM), jnp.float32),
                          jnp.cumsum(uf, axis=1)], axis=1)
    win = jnp.array(POOL_WINDOWS, jnp.int32)
    t1 = jnp.arange(1, S + 1, dtype=jnp.int32)[:, None]
    lo = jnp.maximum(t1 - win[None, :], 0)
    cnt = (t1 - lo).astype(jnp.float32)
    lower = cs[:, lo, jnp.arange(POOL_GROUPS)[None, :]]
    mean = (cs[:, 1:] - lower) / cnt[None, :, :, None]
    diff = (mean - uf).astype(u.dtype)
    y = jnp.einsum('bsgc,gcd->bsgd', diff, pool_w)
    return y.reshape(B, S, POOL_WIDTH) * pool_scale


def causal_depthwise_conv(x, w):
    C = x.shape[-1]
    return lax.conv_general_dilated(x, w[:, None, :], window_strides=(1,),
                                    padding=[(CONV_WIDTH - 1, 0)],
                                    dimension_numbers=('NWC', 'WIO', 'NWC'),
                                    feature_group_count=C)


def chunked_gated_delta_rule(q, k, v, g, beta):
    B, H, S, dk = q.shape
    dv = v.shape[-1]
    n = S // CHUNK

    def chunks(t):
        return t.reshape(B, H, n, CHUNK, *t.shape[3:])

    q = chunks(q * dk ** -0.5)
    k = chunks(k)
    v = chunks(v)
    beta = chunks(beta)
    g = jnp.cumsum(chunks(g), axis=-1)
    incl = jnp.tril(jnp.ones((CHUNK, CHUNK), bool))
    strict = jnp.tril(jnp.ones((CHUNK, CHUNK), bool), -1)
    diff = g[..., :, None] - g[..., None, :]
    decay = jnp.where(incl, jnp.exp(jnp.where(incl, diff, 0.0)), 0.0)
    k_beta = k * beta[..., None]
    a_mat = jnp.eye(CHUNK, dtype=q.dtype) + jnp.where(
        strict, jnp.einsum('bhnid,bhnjd->bhnij', k_beta, k) * decay, 0.0)
    solve = functools.partial(lax.linalg.triangular_solve, left_side=True, lower=True,
                              unit_diagonal=True)
    u = solve(a_mat, v * beta[..., None])
    w = solve(a_mat, k_beta * jnp.exp(g)[..., None])
    qk = jnp.where(incl, jnp.einsum('bhnid,bhnjd->bhnij', q, k) * decay, 0.0)

    def step(state, xs):
        q_c, k_c, u_c, w_c, qk_c, g_c = xs
        v_new = u_c - jnp.einsum('bhcd,bhde->bhce', w_c, state)
        o_c = (jnp.einsum('bhcd,bhde->bhce', q_c * jnp.exp(g_c)[..., None], state)
               + jnp.einsum('bhij,bhje->bhie', qk_c, v_new))
        g_last = g_c[..., -1:]
        state = (state * jnp.exp(g_last)[..., None]
                 + jnp.einsum('bhcd,bhce->bhde', k_c * jnp.exp(g_last - g_c)[..., None], v_new))
        return state, o_c

    xs = tuple(jnp.moveaxis(t, 2, 0) for t in (q, k, u, w, qk, g))
    state0 = jnp.zeros((B, H, dk, dv), jnp.float32)
    _, o = lax.scan(step, state0, xs)
    return jnp.moveaxis(o, 0, 2).reshape(B, H, S, dv)


def gated_deltanet(qkv, z, beta_logit, a_logit, conv_w, a_log, dt_bias, o_norm_g):
    B, S, _ = qkv.shape
    dtype = qkv.dtype
    qkv = jax.nn.silu(causal_depthwise_conv(qkv, conv_w))
    q, k, v = jnp.split(qkv, 3, axis=-1)

    def heads(t):
        return t.reshape(B, S, DN_HEADS, DN_HEAD_DIM).transpose(0, 2, 1, 3).astype(jnp.float32)

    q = l2norm(heads(q))
    k = l2norm(heads(k))
    v = heads(v)
    beta = jax.nn.sigmoid(beta_logit.astype(jnp.float32)).transpose(0, 2, 1)
    g = (-jnp.exp(a_log.astype(jnp.float32))
         * jax.nn.softplus(a_logit.astype(jnp.float32) + dt_bias.astype(jnp.float32))
         ).transpose(0, 2, 1)
    o = chunked_gated_delta_rule(q, k, v, g, beta).transpose(0, 2, 1, 3)
    o = o * lax.rsqrt(jnp.mean(o * o, axis=-1, keepdims=True) + EPS) * o_norm_g.astype(jnp.float32)
    o = o * jax.nn.silu(z.reshape(B, S, DN_HEADS, DN_HEAD_DIM).astype(jnp.float32))
    return o.reshape(B, S, DN_WIDTH).astype(dtype)


def hybrid_mixer(h, w_in, pool_w, pool_scale, conv_w, a_log, dt_bias, o_norm_g, w_out):
    proj = h @ w_in
    u, qkv, z, beta_logit, a_logit = jnp.split(
        proj, [POOL_WIDTH, POOL_WIDTH + 3 * DN_WIDTH, POOL_WIDTH + 4 * DN_WIDTH,
               POOL_WIDTH + 4 * DN_WIDTH + DN_HEADS], axis=-1)
    y_pool = pool_mixer(u, pool_w, pool_scale)
    y_dn = gated_deltanet(qkv, z, beta_logit, a_logit, conv_w, a_log, dt_bias, o_norm_g)
    return jnp.concatenate([y_pool, y_dn], axis=-1) @ w_out


def routed_experts(ht, expert_idx, expert_w, w_gate, w_up, w_down):
    T, D = ht.shape
    A = T * TOP_K
    n_blocks = -(-(A + N_EXPERTS * (MOE_BLOCK - 1)) // MOE_BLOCK)
    n_pad = n_blocks * MOE_BLOCK
    flat_e = expert_idx.reshape(A)
    flat_tok = jnp.repeat(jnp.arange(T, dtype=jnp.int32), TOP_K)
    flat_w = expert_w.reshape(A)
    order = jnp.argsort(flat_e)
    e_sorted = flat_e[order]
    counts = jnp.bincount(flat_e, length=N_EXPERTS)
    padded = (counts + MOE_BLOCK - 1) // MOE_BLOCK * MOE_BLOCK
    pad_end = jnp.cumsum(padded)
    pad_start = pad_end - padded
    start = jnp.cumsum(counts) - counts
    dest = pad_start[e_sorted] + jnp.arange(A, dtype=jnp.int32) - start[e_sorted]
    buf_tok = jnp.zeros((n_pad,), jnp.int32).at[dest].set(flat_tok[order])
    buf_w = jnp.zeros((n_pad,), ht.dtype).at[dest].set(flat_w[order])
    block_start = jnp.arange(n_blocks, dtype=pad_end.dtype) * MOE_BLOCK
    block_e = jnp.minimum(jnp.searchsorted(pad_end, block_start, side='right'), N_EXPERTS - 1)

    def block_fn(args):
        tok_b, w_b, e_b = args
        xb = ht[tok_b]
        hid = jax.nn.silu(xb @ w_gate[e_b]) * (xb @ w_up[e_b])
        return (hid @ w_down[e_b]) * w_b[:, None]

    ys = lax.map(block_fn, (buf_tok.reshape(n_blocks, MOE_BLOCK),
                            buf_w.reshape(n_blocks, MOE_BLOCK), block_e))
    return jnp.zeros((T, D), ht.dtype).at[buf_tok].add(ys.reshape(n_pad, D))


def hierarchical_moe(h, w_rg, b_rg, w_re, b_re, w_gate, w_up, w_down):
    B, S, D = h.shape
    T = B * S
    ht = h.reshape(T, D)
    hf = ht.astype(jnp.float32)
    p_group = jax.nn.softmax(hf @ w_rg.astype(jnp.float32) + b_rg.astype(jnp.float32), axis=-1)
    p_top, g_idx = lax.top_k(p_group, 1)
    logit_e = (hf @ w_re.astype(jnp.float32) + b_re.astype(jnp.float32)).reshape(
        T, N_GROUPS, EXPERTS_PER_GROUP)
    logit_in = logit_e[jnp.arange(T), g_idx[:, 0]]
    top_logit, e_local = lax.top_k(logit_in, TOP_K)
    weights = p_top * jax.nn.softmax(top_logit, axis=-1)
    expert_idx = g_idx * EXPERTS_PER_GROUP + e_local
    y = routed_experts(ht, expert_idx, weights.astype(h.dtype), w_gate, w_up, w_down)
    return y.reshape(B, S, D)


def setup_inputs(seed: int = 0) -> dict:
    key = jax.random.key(seed)
    ks = jax.random.split(key, 24)
    f32 = jnp.float32
    L = DEPTH

    def nrm(k, shape, scale):
        return jax.random.normal(k, shape, f32) * scale

    dt = jnp.exp(jax.random.uniform(ks[10], (L, DN_HEADS), f32,
                                    minval=math.log(1e-3), maxval=math.log(1e-1)))
    return {
        "x": nrm(ks[0], (BATCH, SEQ, D_MODEL), 1.0),
        "c": nrm(ks[1], (BATCH, D_MODEL), 1.0),
        "w_ada": nrm(ks[2], (L, D_MODEL, 6 * D_MODEL), D_MODEL ** -0.5),
        "b_ada": nrm(ks[3], (L, 6 * D_MODEL), 0.02),
        "norm1_g": 1.0 + nrm(ks[4], (L, D_MODEL), 0.02),
        "w_in": nrm(ks[5], (L, D_MODEL, IN_COLS), D_MODEL ** -0.5),
        "pool_w": nrm(ks[6], (L, POOL_GROUPS, POOL_GROUP_DIM, POOL_GROUP_DIM), POOL_GROUP_DIM ** -0.5),
        "pool_scale": 1.0 + nrm(ks[7], (L, POOL_WIDTH), 0.02),
        "conv_w": nrm(ks[8], (L, CONV_WIDTH, 3 * DN_WIDTH), CONV_WIDTH ** -0.5),
        "a_log": jnp.log(jax.random.uniform(ks[9], (L, DN_HEADS), f32, minval=1.0, maxval=16.0)),
        "dt_bias": dt + jnp.log(-jnp.expm1(-dt)),
        "o_norm_g": 1.0 + nrm(ks[11], (L, DN_HEAD_DIM), 0.02),
        "w_out": nrm(ks[12], (L, MIX_WIDTH, D_MODEL), MIX_WIDTH ** -0.5),
        "norm2_g": 1.0 + nrm(ks[13], (L, D_MODEL), 0.02),
        "w_router_group": nrm(ks[14], (L, D_MODEL, N_GROUPS), D_MODEL ** -0.5),
        "b_router_group": nrm(ks[15], (L, N_GROUPS), 0.01),
        "w_router_expert": nrm(ks[16], (L, D_MODEL, N_EXPERTS), D_MODEL ** -0.5),
        "b_router_expert": nrm(ks[17], (L, N_EXPERTS), 0.01),
        "w_gate": nrm(ks[18], (L, N_EXPERTS, D_MODEL, D_EXPERT), D_MODEL ** -0.5),
        "w_up": nrm(ks[19], (L, N_EXPERTS, D_MODEL, D_EXPERT), D_MODEL ** -0.5),
        "w_down": nrm(ks[20], (L, N_EXPERTS, D_EXPERT, D_MODEL), D_EXPERT ** -0.5),
        "norm_f_g": 1.0 + nrm(ks[21], (D_MODEL,), 0.02),
    }


def reference(x, c, w_ada, b_ada, norm1_g, w_in, pool_w, pool_scale, conv_w, a_log, dt_bias,
              o_norm_g, w_out, norm2_g, w_router_group, b_router_group, w_router_expert,
              b_router_expert, w_gate, w_up, w_down, norm_f_g):
    c_act = jax.nn.silu(c)
    for l in range(DEPTH):
        mod = c_act @ w_ada[l] + b_ada[l]
        shift1, scale1, gate1, shift2, scale2, gate2 = [m[:, None, :] for m in jnp.split(mod, 6, axis=-1)]
        h = rmsnorm(x, norm1_g[l]) * (1.0 + scale1) + shift1
        x = x + gate1 * hybrid_mixer(h, w_in[l], pool_w[l], pool_scale[l], conv_w[l], a_log[l],
                                     dt_bias[l], o_norm_g[l], w_out[l])
        h = rmsnorm(x, norm2_g[l]) * (1.0 + scale2) + shift2
        x = x + gate2 * hierarchical_moe(h, w_router_group[l], b_router_group[l], w_router_expert[l],
                                         b_router_expert[l], w_gate[l], w_up[l], w_down[l])
    return rmsnorm(x, norm_f_g)
```

```python
import functools

import jax
import jax.numpy as jnp
from jax import lax
from jax.experimental import pallas as pl
from jax.experimental.pallas import tpu as pltpu

F32 = jnp.float32
BF16 = jnp.bfloat16

POOL_GROUPS = 4
POOL_WINDOWS = (2, 4, 8, 16)
POOL_HALO = 16
DN_HEADS = 8
DN_HEAD_DIM = 128
CONV_WIDTH = 4
DN_CHUNK = 128
N_GROUPS = 4
EXPERTS_PER_GROUP = 8
N_EXPERTS = N_GROUPS * EXPERTS_PER_GROUP
TOP_K = 2
MOE_BLOCK = 256
EPS = 1e-6
LANES = 128
NEG_BIG = -3.0e38
VMEM_LIMIT = 60 * 1024 * 1024


def _silu(x):
    return x * jax.nn.sigmoid(x)


def _dot(a, b):
    return jnp.dot(a, b, preferred_element_type=F32)


def _cparams(*sem):
    return pltpu.CompilerParams(dimension_semantics=sem, vmem_limit_bytes=VMEM_LIMIT)


def _ada_kernel(c_ref, w_ref, b_ref, o_ref):
    ca = _silu(c_ref[...])
    o_ref[...] = jnp.dot(ca, w_ref[...], preferred_element_type=F32,
                         precision=lax.Precision.HIGHEST) + b_ref[...]


def _ada(c, w_ada, b_ada):
    B, D = c.shape
    N = w_ada.shape[1]
    tn = 1024
    return pl.pallas_call(
        _ada_kernel,
        grid=(N // tn,),
        in_specs=[pl.BlockSpec((B, D), lambda j: (0, 0)),
                  pl.BlockSpec((D, tn), lambda j: (0, j)),
                  pl.BlockSpec((1, tn), lambda j: (0, j))],
        out_specs=pl.BlockSpec((B, tn), lambda j: (0, j)),
        out_shape=jax.ShapeDtypeStruct((B, N), F32),
        compiler_params=_cparams("arbitrary"),
        name="ada",
    )(c, w_ada, b_ada.reshape(1, N))


def _inproj_kernel(x_ref, mod_ref, g_ref, w_ref, wba_ref, proj_ref, ba_ref, h_ref):
    @pl.when(pl.program_id(1) == 0)
    def _():
        x = x_ref[...]
        ms = jnp.mean(x * x, axis=-1, keepdims=True)
        y = x * lax.rsqrt(ms + EPS) * g_ref[...]
        h = (y * (1.0 + mod_ref[0, 1:2, :]) + mod_ref[0, 0:1, :]).astype(BF16)
        h_ref[...] = h
        ba_ref[...] = _dot(h, wba_ref[...])

    proj_ref[...] = _dot(h_ref[...], w_ref[...])


def _inproj(x2d, mod3, norm_g, w_main, w_ba, seq):
    T, D = x2d.shape
    N = w_main.shape[1]
    tm, tn = 1024, 1024
    return pl.pallas_call(
        _inproj_kernel,
        grid=(T // tm, N // tn),
        in_specs=[pl.BlockSpec((tm, D), lambda i, j: (i, 0)),
                  pl.BlockSpec((1, 6, D), lambda i, j: (i * tm // seq, 0, 0)),
                  pl.BlockSpec((1, D), lambda i, j: (0, 0)),
                  pl.BlockSpec((D, tn), lambda i, j: (0, j)),
                  pl.BlockSpec((D, LANES), lambda i, j: (0, 0))],
        out_specs=[pl.BlockSpec((tm, tn), lambda i, j: (i, j)),
                   pl.BlockSpec((tm, LANES), lambda i, j: (i, 0))],
        out_shape=[jax.ShapeDtypeStruct((T, N), F32),
                   jax.ShapeDtypeStruct((T, LANES), F32)],
        scratch_shapes=[pltpu.VMEM((tm, D), BF16)],
        compiler_params=_cparams("arbitrary", "arbitrary"),
        name="inproj",
    )(x2d, mod3, norm_g.reshape(1, D), w_main, w_ba)


def _pool_kernel(u_ref, halo_ref, w_ref, sc_ref, o_ref, *, ts, gd):
    i = pl.program_id(1)
    halo = jnp.where(i > 0, halo_ref[...], 0.0)
    tpos = lax.broadcasted_iota(jnp.int32, (ts, 1), 0) + i * ts + 1
    for g in range(POOL_GROUPS):
        win = POOL_WINDOWS[g]
        u = u_ref[:, g * gd:(g + 1) * gd]
        s = jnp.concatenate([halo[:, g * gd:(g + 1) * gd], u], axis=0)
        k = 1
        while k < win:
            s = s + pltpu.roll(s, k, axis=0)
            k *= 2
        cnt = jnp.minimum(tpos, win).astype(F32)
        diff = s[POOL_HALO:, :] / cnt - u
        y = _dot(diff.astype(BF16), w_ref[g].astype(BF16)) * sc_ref[:, g * gd:(g + 1) * gd]
        o_ref[:, g * gd:(g + 1) * gd] = y.astype(o_ref.dtype)


def _pool(proj, pool_w, pool_scale, batch, seq):
    T = proj.shape[0]
    G, gd, _ = pool_w.shape
    width = G * gd
    ts = 512
    nt = seq // ts
    hb = ts // POOL_HALO
    return pl.pallas_call(
        functools.partial(_pool_kernel, ts=ts, gd=gd),
        grid=(batch, nt),
        in_specs=[pl.BlockSpec((ts, width), lambda b, i: (b * nt + i, 0)),
                  pl.BlockSpec((POOL_HALO, width),
                               lambda b, i: (jnp.maximum((b * nt + i) * hb - 1, 0), 0)),
                  pl.BlockSpec((G, gd, gd), lambda b, i: (0, 0, 0)),
                  pl.BlockSpec((1, width), lambda b, i: (0, 0))],
        out_specs=pl.BlockSpec((ts, width), lambda b, i: (b * nt + i, 0)),
        out_shape=jax.ShapeDtypeStruct((T, width), BF16),
        compiler_params=_cparams("arbitrary", "arbitrary"),
        name="pool",
    )(proj, proj, pool_w, pool_scale.reshape(1, width))


def _bmm(a, b):
    return jnp.einsum('nij,njk->nik', a.astype(BF16), b.astype(BF16),
                      preferred_element_type=F32)


def _bmm_nt(a, b):
    return jnp.einsum('nid,njd->nij', a.astype(BF16), b.astype(BF16),
                      preferred_element_type=F32)


def _dn_kernel(alog_ref, dtb_ref, q_ref, k_ref, v_ref, z_ref, ba_ref,
               cwq_ref, cwk_ref, cwv_ref, on_ref, o_ref, o_scr, *, seq, chunk):
    h = pl.program_id(1)
    d = DN_HEAD_DIM
    n = seq // chunk
    row = lax.broadcasted_iota(jnp.int32, (seq, d), 0)

    def conv_silu(x_ref, cw_ref):
        x = x_ref[...]
        acc = x * cw_ref[CONV_WIDTH - 1:CONV_WIDTH, :]
        for j in range(CONV_WIDTH - 1):
            sh = CONV_WIDTH - 1 - j
            acc = acc + jnp.where(row >= sh, pltpu.roll(x, sh, axis=0), 0.0) * cw_ref[j:j + 1, :]
        return _silu(acc)

    def l2n(x):
        return x * lax.rsqrt(jnp.sum(x * x, axis=-1, keepdims=True) + EPS)

    qn = l2n(conv_silu(q_ref, cwq_ref)) * (d ** -0.5)
    kn = l2n(conv_silu(k_ref, cwk_ref))
    v = conv_silu(v_ref, cwv_ref)

    lane = lax.broadcasted_iota(jnp.int32, (seq, LANES), 1)
    ba = ba_ref[...]
    beta_logit = jnp.sum(jnp.where(lane == h, ba, 0.0), axis=-1, keepdims=True)
    a_logit = jnp.sum(jnp.where(lane == h + DN_HEADS, ba, 0.0), axis=-1, keepdims=True)
    beta = jnp.broadcast_to(jax.nn.sigmoid(beta_logit), (seq, d))
    a = a_logit + dtb_ref[h]
    softplus = jnp.maximum(a, 0.0) + jnp.log1p(jnp.exp(-jnp.abs(a)))
    neg_rate = -jnp.exp(jnp.full((1, 1), alog_ref[h], F32))
    gc = jnp.broadcast_to(neg_rate * softplus, (seq, d))
    rmod = row & (chunk - 1)
    k_ = 1
    while k_ < chunk:
        gc = gc + jnp.where(rmod >= k_, pltpu.roll(gc, k_, axis=0), 0.0)
        k_ *= 2

    gc3 = gc.reshape(n, chunk, d)
    ii = lax.broadcasted_iota(jnp.int32, (chunk, chunk), 0)
    jj = lax.broadcasted_iota(jnp.int32, (chunk, chunk), 1)
    incl = (ii >= jj)[None]
    strict = (ii > jj)[None]
    diff = gc3 - jnp.swapaxes(gc3, 1, 2)
    decay = jnp.where(incl, jnp.exp(jnp.where(incl, diff, 0.0)), 0.0)

    kb = kn * beta
    kn3 = kn.reshape(n, chunk, d)
    kb3 = kb.reshape(n, chunk, d)
    lmat = jnp.where(strict, _bmm_nt(kb3, kn3) * decay, 0.0)
    qk = jnp.where(incl, _bmm_nt(qn.reshape(n, chunk, d), kn3) * decay, 0.0)

    def merge_mask(lv):
        same = (ii >> (lv + 1)) == (jj >> (lv + 1))
        return (same & (((ii >> lv) & 1) == 1) & (((jj >> lv) & 1) == 0))[None]

    tinv = (ii == jj).astype(F32)[None] - jnp.where(merge_mask(0), lmat, 0.0)
    for lv in range(1, chunk.bit_length() - 1):
        a21 = jnp.where(merge_mask(lv), lmat, 0.0)
        tinv = tinv - _bmm(tinv, _bmm(a21, tinv))

    egc = jnp.exp(gc)
    rhs = jnp.concatenate([v * beta, kb * egc], axis=-1).reshape(n, chunk, 2 * d)
    uw = _bmm(tinv, rhs)
    qg = (qn * egc).reshape(n, chunk, d)
    glast = gc3[:, chunk - 1:chunk, :]
    kdt = jnp.swapaxes(kn3 * jnp.exp(glast - gc3), 1, 2)
    eglast = jnp.exp(glast)

    state = jnp.zeros((d, d), F32)
    for c in range(n):
        sb = state.astype(BF16)
        vnew = uw[c, :, :d] - _dot(uw[c, :, d:].astype(BF16), sb)
        vb = vnew.astype(BF16)
        o_scr[c * chunk:(c + 1) * chunk, :] = (_dot(qg[c].astype(BF16), sb)
                                               + _dot(qk[c].astype(BF16), vb))
        state = state * eglast[c] + _dot(kdt[c].astype(BF16), vb)

    o = o_scr[...]
    o = o * lax.rsqrt(jnp.mean(o * o, axis=-1, keepdims=True) + EPS) * on_ref[...]
    o_ref[...] = (o * _silu(z_ref[...])).astype(o_ref.dtype)


def _deltanet(proj, ba, conv_w, a_log, dt_bias, o_norm_g, batch, seq, col0):
    T = proj.shape[0]
    d = DN_HEAD_DIM
    H = DN_HEADS
    cb = col0 // d
    smem = pl.BlockSpec(memory_space=pltpu.SMEM)

    def colspec(off):
        return pl.BlockSpec((seq, d), lambda b, h: (b, off + h))

    def convspec(off):
        return pl.BlockSpec((CONV_WIDTH, d), lambda b, h: (0, off + h))

    return pl.pallas_call(
        functools.partial(_dn_kernel, seq=seq, chunk=DN_CHUNK),
        grid=(batch, H),
        in_specs=[smem, smem,
                  colspec(cb), colspec(cb + H), colspec(cb + 2 * H), colspec(cb + 3 * H),
                  pl.BlockSpec((seq, LANES), lambda b, h: (b, 0)),
                  convspec(0), convspec(H), convspec(2 * H),
                  pl.BlockSpec((1, d), lambda b, h: (0, 0))],
        out_specs=pl.BlockSpec((seq, d), lambda b, h: (b, h)),
        out_shape=jax.ShapeDtypeStruct((T, H * d), BF16),
        scratch_shapes=[pltpu.VMEM((seq, d), F32)],
        compiler_params=_cparams("arbitrary", "arbitrary"),
        name="deltanet",
    )(a_log, dt_bias, proj, proj, proj, proj, ba, conv_w, conv_w, conv_w,
      o_norm_g.reshape(1, d))


def _route(logits):
    lane = lax.broadcasted_iota(jnp.int32, logits.shape, 1)
    lanef = lane.astype(F32)
    far = float(LANES)

    def first_max(vals):
        m = jnp.max(vals, axis=-1, keepdims=True)
        idx = jnp.min(jnp.where(vals == m, lanef, far), axis=-1, keepdims=True)
        return m, idx

    gl = jnp.where(lane < N_GROUPS, logits, NEG_BIG)
    gmax, gidx = first_max(gl)
    p_top = 1.0 / jnp.sum(jnp.exp(gl - gmax), axis=-1, keepdims=True)
    lo = N_GROUPS + EXPERTS_PER_GROUP * gidx
    el = jnp.where(lanef >= lo, jnp.where(lanef < lo + EXPERTS_PER_GROUP, logits, NEG_BIG), NEG_BIG)
    m1, i1 = first_max(el)
    m2, i2 = first_max(jnp.where(lanef == i1, NEG_BIG, el))
    t = jnp.exp(m2 - m1)
    w1 = p_top / (1.0 + t)
    w2 = w1 * t
    return jnp.where(lane == 0, i1 - N_GROUPS,
                     jnp.where(lane == 1, i2 - N_GROUPS,
                               jnp.where(lane == 2, w1, jnp.where(lane == 3, w2, 0.0))))


def _outproj_kernel(yp_ref, yd_ref, wo_ref, x_ref, mod_ref, g_ref, wr_ref, br_ref,
                    x2_ref, h2_ref, route_ref):
    half = yp_ref.shape[1]
    out = _dot(yp_ref[...], wo_ref[:half, :]) + _dot(yd_ref[...], wo_ref[half:, :])
    x2 = x_ref[...] + mod_ref[0, 2:3, :] * out
    x2_ref[...] = x2
    ms = jnp.mean(x2 * x2, axis=-1, keepdims=True)
    y = x2 * lax.rsqrt(ms + EPS) * g_ref[...]
    h2 = y * (1.0 + mod_ref[0, 4:5, :]) + mod_ref[0, 3:4, :]
    h2_ref[...] = h2
    hi = h2.astype(BF16)
    lo = (h2 - hi.astype(F32)).astype(BF16)
    wr = wr_ref[...]
    whi = wr.astype(BF16)
    wlo = (wr - whi.astype(F32)).astype(BF16)
    logits = _dot(hi, whi) + (_dot(lo, whi) + _dot(hi, wlo)) + br_ref[...]
    route_ref[...] = _route(logits)


def _outproj(y_pool, y_dn, w_out_bf, x2d, mod3, norm_g, w_router, b_router, seq):
    T, D = x2d.shape
    half = y_pool.shape[1]
    tm = 256
    row = lambda i: (i, 0)
    const = lambda i: (0, 0)
    return pl.pallas_call(
        _outproj_kernel,
        grid=(T // tm,),
        in_specs=[pl.BlockSpec((tm, half), row),
                  pl.BlockSpec((tm, half), row),
                  pl.BlockSpec((2 * half, D), const),
                  pl.BlockSpec((tm, D), row),
                  pl.BlockSpec((1, 6, D), lambda i: (i * tm // seq, 0, 0)),
                  pl.BlockSpec((1, D), const),
                  pl.BlockSpec((D, LANES), const),
                  pl.BlockSpec((1, LANES), const)],
        out_specs=[pl.BlockSpec((tm, D), row),
                   pl.BlockSpec((tm, D), row),
                   pl.BlockSpec((tm, LANES), row)],
        out_shape=[jax.ShapeDtypeStruct((T, D), F32),
                   jax.ShapeDtypeStruct((T, D), F32),
                   jax.ShapeDtypeStruct((T, LANES), F32)],
        compiler_params=_cparams("arbitrary"),
        name="outproj",
    )(y_pool, y_dn, w_out_bf, x2d, mod3, norm_g.reshape(1, D), w_router, b_router)


def _row_gather(idx_ref, nrows, src_hbm, dst, sem):
    def body(r, carry):
        t = idx_ref[0, 0, r]
        pltpu.make_async_copy(src_hbm.at[pl.ds(t, 1), :], dst.at[pl.ds(r, 1), :], sem).start()
        return carry
    lax.fori_loop(0, nrows, body, 0)


def _row_gather_wait(nrows, src_hbm, dst, sem):
    def body(r, carry):
        pltpu.make_async_copy(src_hbm.at[pl.ds(0, 1), :], dst.at[pl.ds(r, 1), :], sem).wait()
        return carry
    lax.fori_loop(0, nrows, body, 0)


def _expert_kernel(be_ref, nact_ref, tok_ref, tokn_ref, h_hbm, wg_ref, wu_ref, wd_ref,
                   ys_ref, xbuf, sem, *, bm):
    i = pl.program_id(0)
    nact = nact_ref[0]
    slot = i % 2

    @pl.when(i == 0)
    def _():
        _row_gather(tok_ref, bm, h_hbm, xbuf.at[0], sem.at[0])

    @pl.when(i + 1 < nact)
    def _():
        _row_gather(tokn_ref, bm, h_hbm, xbuf.at[1 - slot], sem.at[1 - slot])

    @pl.when(i < nact)
    def _():
        _row_gather_wait(bm, h_hbm, xbuf.at[slot], sem.at[slot])
        xb = xbuf[slot].astype(BF16)
        gate = _dot(xb, wg_ref[0].astype(BF16))
        up = _dot(xb, wu_ref[0].astype(BF16))
        hid = (_silu(gate) * up).astype(BF16)
        ys_ref[...] = _dot(hid, wd_ref[0].astype(BF16))

    @pl.when(i >= nact)
    def _():
        ys_ref[...] = jnp.zeros_like(ys_ref)


def _experts(h2, block_e, nact, buf_tok, w_gate, w_up, w_down):
    T, D = h2.shape
    E, _, De = w_gate.shape
    bm = MOE_BLOCK
    n_pad = buf_tok.shape[0]
    nb = n_pad // bm
    tok3 = buf_tok.reshape(nb, 1, bm)
    grid_spec = pltpu.PrefetchScalarGridSpec(
        num_scalar_prefetch=2,
        grid=(nb,),
        in_specs=[pl.BlockSpec((1, 1, bm), lambda i, be, na: (i, 0, 0), memory_space=pltpu.SMEM),
                  pl.BlockSpec((1, 1, bm), lambda i, be, na: (jnp.minimum(i + 1, nb - 1), 0, 0),
                               memory_space=pltpu.SMEM),
                  pl.BlockSpec(memory_space=pl.ANY),
                  pl.BlockSpec((1, D, De), lambda i, be, na: (be[i], 0, 0)),
                  pl.BlockSpec((1, D, De), lambda i, be, na: (be[i], 0, 0)),
                  pl.BlockSpec((1, De, D), lambda i, be, na: (be[i], 0, 0))],
        out_specs=pl.BlockSpec((bm, D), lambda i, be, na: (i, 0)),
        scratch_shapes=[pltpu.VMEM((2, bm, D), F32), pltpu.SemaphoreType.DMA((2,))],
    )
    return pl.pallas_call(
        functools.partial(_expert_kernel, bm=bm),
        grid_spec=grid_spec,
        out_shape=jax.ShapeDtypeStruct((n_pad, D), F32),
        compiler_params=_cparams("arbitrary"),
        name="experts",
    )(block_e, nact, tok3, tok3, h2, w_gate, w_up, w_down)


def _combine_kernel(pos_ref, posn_ref, ys_hbm, x2_ref, mod_ref, route_ref, g_ref, o_ref,
                    ybuf, sem, *, tm):
    i = pl.program_id(0)
    nsteps = pl.num_programs(0)
    slot = i % 2
    nrows = TOP_K * tm

    @pl.when(i == 0)
    def _():
        _row_gather(pos_ref, nrows, ys_hbm, ybuf.at[0], sem.at[0])

    @pl.when(i + 1 < nsteps)
    def _():
        _row_gather(posn_ref, nrows, ys_hbm, ybuf.at[1 - slot], sem.at[1 - slot])

    _row_gather_wait(nrows, ys_hbm, ybuf.at[slot], sem.at[slot])
    route = route_ref[...]
    y = route[:, 2:3] * ybuf[slot, :tm, :] + route[:, 3:4] * ybuf[slot, tm:, :]
    x3 = x2_ref[...] + mod_ref[0, 5:6, :] * y
    ms = jnp.mean(x3 * x3, axis=-1, keepdims=True)
    o_ref[...] = x3 * lax.rsqrt(ms + EPS) * g_ref[...]


def _combine(ys, pos, x2, mod3, route, norm_g, seq):
    T, D = x2.shape
    tm = 256
    nt = T // tm
    pos3 = pos.reshape(nt, tm, TOP_K).transpose(0, 2, 1).reshape(nt, 1, TOP_K * tm)
    row = lambda i: (i, 0)
    return pl.pallas_call(
        functools.partial(_combine_kernel, tm=tm),
        grid=(nt,),
        in_specs=[pl.BlockSpec((1, 1, TOP_K * tm), lambda i: (i, 0, 0), memory_space=pltpu.SMEM),
                  pl.BlockSpec((1, 1, TOP_K * tm), lambda i: (jnp.minimum(i + 1, nt - 1), 0, 0),
                               memory_space=pltpu.SMEM),
                  pl.BlockSpec(memory_space=pl.ANY),
                  pl.BlockSpec((tm, D), row),
                  pl.BlockSpec((1, 6, D), lambda i: (i * tm // seq, 0, 0)),
                  pl.BlockSpec((tm, LANES), row),
                  pl.BlockSpec((1, D), lambda i: (0, 0))],
        out_specs=pl.BlockSpec((tm, D), row),
        out_shape=jax.ShapeDtypeStruct((T, D), F32),
        scratch_shapes=[pltpu.VMEM((2, TOP_K * tm, D), F32), pltpu.SemaphoreType.DMA((2,))],
        compiler_params=_cparams("arbitrary"),
        name="combine",
    )(pos3, pos3, ys, x2, mod3, route, norm_g.reshape(1, D))


def _dispatch_plan(route, n_tokens):
    A = n_tokens * TOP_K
    bm = MOE_BLOCK
    nb = -(-(A + N_EXPERTS * (bm - 1)) // bm)
    flat_e = route[:, :TOP_K].astype(jnp.int32).reshape(A)
    order = jnp.argsort(flat_e)
    e_sorted = flat_e[order]
    counts = jnp.bincount(flat_e, length=N_EXPERTS).astype(jnp.int32)
    padded = (counts + bm - 1) // bm * bm
    pad_end = jnp.cumsum(padded)
    pad_start = pad_end - padded
    start = jnp.cumsum(counts) - counts
    dest = pad_start[e_sorted] + jnp.arange(A, dtype=jnp.int32) - start[e_sorted]
    buf_tok = jnp.zeros((nb * bm,), jnp.int32).at[dest].set((order // TOP_K).astype(jnp.int32))
    pos = jnp.zeros((A,), jnp.int32).at[order].set(dest)
    block_start = jnp.arange(nb, dtype=jnp.int32) * bm
    block_e = jnp.minimum(jnp.searchsorted(pad_end, block_start, side='right'),
                          N_EXPERTS - 1).astype(jnp.int32)
    nact = (pad_end[-1:] // bm).astype(jnp.int32)
    return buf_tok, pos, block_e, nact


def kernel(x, c, w_ada, b_ada, norm1_g, w_in, pool_w, pool_scale, conv_w, a_log, dt_bias,
           o_norm_g, w_out, norm2_g, w_router_group, b_router_group, w_router_expert,
           b_router_expert, w_gate, w_up, w_down, norm_f_g):
    B, S, D = x.shape
    T = B * S
    depth = w_ada.shape[0]
    pool_width = pool_w.shape[1] * pool_w.shape[2]
    n_main = pool_width + 4 * DN_HEADS * DN_HEAD_DIM
    n_route = N_GROUPS + N_EXPERTS

    assert depth == 1, "kernel supports the single-layer configuration only"
    l = 0
    xt = x.reshape(T, D)
    mod3 = _ada(c, w_ada[l], b_ada[l]).reshape(B, 6, D)

    w_main = w_in[l, :, :n_main].astype(BF16)
    w_ba = jnp.pad(w_in[l, :, n_main:], ((0, 0), (0, LANES - 2 * DN_HEADS))).astype(BF16)
    proj, ba = _inproj(xt, mod3, norm1_g[l], w_main, w_ba, S)

    y_pool = _pool(proj, pool_w[l], pool_scale[l], B, S)
    y_dn = _deltanet(proj, ba, conv_w[l], a_log[l], dt_bias[l], o_norm_g[l], B, S, pool_width)

    w_router = jnp.pad(jnp.concatenate([w_router_group[l], w_router_expert[l]], axis=1),
                       ((0, 0), (0, LANES - n_route)))
    b_router = jnp.pad(jnp.concatenate([b_router_group[l], b_router_expert[l]]),
                       (0, LANES - n_route)).reshape(1, LANES)
    x2, h2, route = _outproj(y_pool, y_dn, w_out[l].astype(BF16), xt, mod3, norm2_g[l],
                             w_router, b_router, S)

    buf_tok, pos, block_e, nact = _dispatch_plan(route, T)
    ys = _experts(h2, block_e, nact, buf_tok, w_gate[l], w_up[l], w_down[l])
    out = _combine(ys, pos, x2, mod3, route, norm_f_g, S)
    return out.reshape(B, S, D)
```

```python
import functools

import jax
import jax.numpy as jnp
from jax import lax
from jax.experimental import pallas as pl
from jax.experimental.pallas import tpu as pltpu

F32 = jnp.float32
BF16 = jnp.bfloat16

POOL_GROUPS = 4
POOL_WINDOWS = (2, 4, 8, 16)
POOL_HALO = 16
DN_HEADS = 8
DN_HEAD_DIM = 128
CONV_WIDTH = 4
DN_CHUNK = 128
N_GROUPS = 4
EXPERTS_PER_GROUP = 8
N_EXPERTS = N_GROUPS * EXPERTS_PER_GROUP
TOP_K = 2
MOE_BLOCK = 256
EPS = 1e-6
LANES = 128
NEG_BIG = -3.0e38
VMEM_LIMIT = 60 * 1024 * 1024


def _silu(x):
    return x * jax.nn.sigmoid(x)


def _dot(a, b):
    return jnp.dot(a, b, preferred_element_type=F32)


def _cparams(*sem):
    return pltpu.CompilerParams(dimension_semantics=sem, vmem_limit_bytes=VMEM_LIMIT)


def _ada_kernel(c_ref, w_ref, b_ref, o_ref):
    ca = _silu(c_ref[...])
    o_ref[...] = jnp.dot(ca, w_ref[...], preferred_element_type=F32,
                         precision=lax.Precision.HIGHEST) + b_ref[...]


def _ada(c, w_ada, b_ada):
    B, D = c.shape
    N = w_ada.shape[1]
    tn = 1024
    return pl.pallas_call(
        _ada_kernel,
        grid=(N // tn,),
        in_specs=[pl.BlockSpec((B, D), lambda j: (0, 0)),
                  pl.BlockSpec((D, tn), lambda j: (0, j)),
                  pl.BlockSpec((1, tn), lambda j: (0, j))],
        out_specs=pl.BlockSpec((B, tn), lambda j: (0, j)),
        out_shape=jax.ShapeDtypeStruct((B, N), F32),
        compiler_params=_cparams("arbitrary"),
        name="ada",
    )(c, w_ada, b_ada.reshape(1, N))


def _inproj_kernel(x_ref, mod_ref, g_ref, w_ref, wba_ref, proj_ref, ba_ref, h_ref):
    @pl.when(pl.program_id(1) == 0)
    def _():
        x = x_ref[...]
        ms = jnp.mean(x * x, axis=-1, keepdims=True)
        y = x * lax.rsqrt(ms + EPS) * g_ref[...]
        h = (y * (1.0 + mod_ref[0, 1:2, :]) + mod_ref[0, 0:1, :]).astype(BF16)
        h_ref[...] = h
        ba_ref[...] = _dot(h, wba_ref[...])

    proj_ref[...] = _dot(h_ref[...], w_ref[...])


def _inproj(x2d, mod3, norm_g, w_main, w_ba, seq):
    T, D = x2d.shape
    N = w_main.shape[1]
    tm, tn = 1024, 1024
    return pl.pallas_call(
        _inproj_kernel,
        grid=(T // tm, N // tn),
        in_specs=[pl.BlockSpec((tm, D), lambda i, j: (i, 0)),
                  pl.BlockSpec((1, 6, D), lambda i, j: (i * tm // seq, 0, 0)),
                  pl.BlockSpec((1, D), lambda i, j: (0, 0)),
                  pl.BlockSpec((D, tn), lambda i, j: (0, j)),
                  pl.BlockSpec((D, LANES), lambda i, j: (0, 0))],
        out_specs=[pl.BlockSpec((tm, tn), lambda i, j: (i, j)),
                   pl.BlockSpec((tm, LANES), lambda i, j: (i, 0))],
        out_shape=[jax.ShapeDtypeStruct((T, N), F32),
                   jax.ShapeDtypeStruct((T, LANES), F32)],
        scratch_shapes=[pltpu.VMEM((tm, D), BF16)],
        compiler_params=_cparams("arbitrary", "arbitrary"),
        name="inproj",
    )(x2d, mod3, norm_g.reshape(1, D), w_main, w_ba)


def _pool_kernel(u_ref, halo_ref, w_ref, sc_ref, o_ref, *, ts, gd):
    i = pl.program_id(1)
    halo = jnp.where(i > 0, halo_ref[...], 0.0)
    tpos = lax.broadcasted_iota(jnp.int32, (ts, 1), 0) + i * ts + 1
    for g in range(POOL_GROUPS):
        win = POOL_WINDOWS[g]
        u = u_ref[:, g * gd:(g + 1) * gd]
        s = jnp.concatenate([halo[:, g * gd:(g + 1) * gd], u], axis=0)
        k = 1
        while k < win:
            s = s + pltpu.roll(s, k, axis=0)
            k *= 2
        cnt = jnp.minimum(tpos, win).astype(F32)
        diff = s[POOL_HALO:, :] / cnt - u
        y = _dot(diff.astype(BF16), w_ref[g].astype(BF16)) * sc_ref[:, g * gd:(g + 1) * gd]
        o_ref[:, g * gd:(g + 1) * gd] = y.astype(o_ref.dtype)


def _pool(proj, pool_w, pool_scale, batch, seq):
    T = proj.shape[0]
    G, gd, _ = pool_w.shape
    width = G * gd
    ts = 512
    nt = seq // ts
    hb = ts // POOL_HALO
    return pl.pallas_call(
        functools.partial(_pool_kernel, ts=ts, gd=gd),
        grid=(batch, nt),
        in_specs=[pl.BlockSpec((ts, width), lambda b, i: (b * nt + i, 0)),
                  pl.BlockSpec((POOL_HALO, width),
                               lambda b, i: (jnp.maximum((b * nt + i) * hb - 1, 0), 0)),
                  pl.BlockSpec((G, gd, gd), lambda b, i: (0, 0, 0)),
                  pl.BlockSpec((1, width), lambda b, i: (0, 0))],
        out_specs=pl.BlockSpec((ts, width), lambda b, i: (b * nt + i, 0)),
        out_shape=jax.ShapeDtypeStruct((T, width), BF16),
        compiler_params=_cparams("arbitrary", "arbitrary"),
        name="pool",
    )(proj, proj, pool_w, pool_scale.reshape(1, width))


def _bmm(a, b):
    return jnp.einsum('nij,njk->nik', a.astype(BF16), b.astype(BF16),
                      preferred_element_type=F32)


def _bmm_nt(a, b):
    return jnp.einsum('nid,njd->nij', a.astype(BF16), b.astype(BF16),
                      preferred_element_type=F32)


def _dn_kernel(alog_ref, dtb_ref, q_ref, k_ref, v_ref, z_ref, ba_ref,
               cwq_ref, cwk_ref, cwv_ref, on_ref, o_ref, o_scr, *, seq, chunk):
    h = pl.program_id(1)
    d = DN_HEAD_DIM
    n = seq // chunk
    row = lax.broadcasted_iota(jnp.int32, (seq, d), 0)

    def conv_silu(x_ref, cw_ref):
        x = x_ref[...]
        acc = x * cw_ref[CONV_WIDTH - 1:CONV_WIDTH, :]
        for j in range(CONV_WIDTH - 1):
            sh = CONV_WIDTH - 1 - j
            acc = acc + jnp.where(row >= sh, pltpu.roll(x, sh, axis=0), 0.0) * cw_ref[j:j + 1, :]
        return _silu(acc)

    def l2n(x):
        return x * lax.rsqrt(jnp.sum(x * x, axis=-1, keepdims=True) + EPS)

    qn = l2n(conv_silu(q_ref, cwq_ref)) * (d ** -0.5)
    kn = l2n(conv_silu(k_ref, cwk_ref))
    v = conv_silu(v_ref, cwv_ref)

    lane = lax.broadcasted_iota(jnp.int32, (seq, LANES), 1)
    ba = ba_ref[...]
    beta_logit = jnp.sum(jnp.where(lane == h, ba, 0.0), axis=-1, keepdims=True)
    a_logit = jnp.sum(jnp.where(lane == h + DN_HEADS, ba, 0.0), axis=-1, keepdims=True)
    beta = jnp.broadcast_to(jax.nn.sigmoid(beta_logit), (seq, d))
    a = a_logit + dtb_ref[h]
    softplus = jnp.maximum(a, 0.0) + jnp.log1p(jnp.exp(-jnp.abs(a)))
    neg_rate = -jnp.exp(jnp.full((1, 1), alog_ref[h], F32))
    gc = jnp.broadcast_to(neg_rate * softplus, (seq, d))
    rmod = row & (chunk - 1)
    k_ = 1
    while k_ < chunk:
        gc = gc + jnp.where(rmod >= k_, pltpu.roll(gc, k_, axis=0), 0.0)
        k_ *= 2

    gc3 = gc.reshape(n, chunk, d)
    ii = lax.broadcasted_iota(jnp.int32, (chunk, chunk), 0)
    jj = lax.broadcasted_iota(jnp.int32, (chunk, chunk), 1)
    incl = (ii >= jj)[None]
    strict = (ii > jj)[None]
    diff = gc3 - jnp.swapaxes(gc3, 1, 2)
    decay = jnp.where(incl, jnp.exp(jnp.where(incl, diff, 0.0)), 0.0)

    kb = kn * beta
    kn3 = kn.reshape(n, chunk, d)
    kb3 = kb.reshape(n, chunk, d)
    lmat = jnp.where(strict, _bmm_nt(kb3, kn3) * decay, 0.0)
    qk = jnp.where(incl, _bmm_nt(qn.reshape(n, chunk, d), kn3) * decay, 0.0)

    def merge_mask(lv):
        same = (ii >> (lv + 1)) == (jj >> (lv + 1))
        return (same & (((ii >> lv) & 1) == 1) & (((jj >> lv) & 1) == 0))[None]

    tinv = (ii == jj).astype(F32)[None] - jnp.where(merge_mask(0), lmat, 0.0)
    for lv in range(1, chunk.bit_length() - 1):
        a21 = jnp.where(merge_mask(lv), lmat, 0.0)
        tinv = tinv - _bmm(tinv, _bmm(a21, tinv))

    egc = jnp.exp(gc)
    rhs = jnp.concatenate([v * beta, kb * egc], axis=-1).reshape(n, chunk, 2 * d)
    uw = _bmm(tinv, rhs)
    qg = (qn * egc).reshape(n, chunk, d)
    glast = gc3[:, chunk - 1:chunk, :]
    kdt = jnp.swapaxes(kn3 * jnp.exp(glast - gc3), 1, 2)
    eglast = jnp.exp(glast)

    state = jnp.zeros((d, d), F32)
    for c in range(n):
        sb = state.astype(BF16)
        vnew = uw[c, :, :d] - _dot(uw[c, :, d:].astype(BF16), sb)
        vb = vnew.astype(BF16)
        o_scr[c * chunk:(c + 1) * chunk, :] = (_dot(qg[c].astype(BF16), sb)
                                               + _dot(qk[c].astype(BF16), vb))
        state = state * eglast[c] + _dot(kdt[c].astype(BF16), vb)

    o = o_scr[...]
    o = o * lax.rsqrt(jnp.mean(o * o, axis=-1, keepdims=True) + EPS) * on_ref[...]
    o_ref[...] = (o * _silu(z_ref[...])).astype(o_ref.dtype)


def _deltanet(proj, ba, conv_w, a_log, dt_bias, o_norm_g, batch, seq, col0):
    T = proj.shape[0]
    d = DN_HEAD_DIM
    H = DN_HEADS
    cb = col0 // d
    smem = pl.BlockSpec(memory_space=pltpu.SMEM)

    def colspec(off):
        return pl.BlockSpec((seq, d), lambda b, h: (b, off + h))

    def convspec(off):
        return pl.BlockSpec((CONV_WIDTH, d), lambda b, h: (0, off + h))

    return pl.pallas_call(
        functools.partial(_dn_kernel, seq=seq, chunk=DN_CHUNK),
        grid=(batch, H),
        in_specs=[smem, smem,
                  colspec(cb), colspec(cb + H), colspec(cb + 2 * H), colspec(cb + 3 * H),
                  pl.BlockSpec((seq, LANES), lambda b, h: (b, 0)),
                  convspec(0), convspec(H), convspec(2 * H),
                  pl.BlockSpec((1, d), lambda b, h: (0, 0))],
        out_specs=pl.BlockSpec((seq, d), lambda b, h: (b, h)),
        out_shape=jax.ShapeDtypeStruct((T, H * d), BF16),
        scratch_shapes=[pltpu.VMEM((seq, d), F32)],
        compiler_params=_cparams("arbitrary", "arbitrary"),
        name="deltanet",
    )(a_log, dt_bias, proj, proj, proj, proj, ba, conv_w, conv_w, conv_w,
      o_norm_g.reshape(1, d))


def _route(logits):
    lane = lax.broadcasted_iota(jnp.int32, logits.shape, 1)
    lanef = lane.astype(F32)
    far = float(LANES)

    def first_max(vals):
        m = jnp.max(vals, axis=-1, keepdims=True)
        idx = jnp.min(jnp.where(vals == m, lanef, far), axis=-1, keepdims=True)
        return m, idx

    gl = jnp.where(lane < N_GROUPS, logits, NEG_BIG)
    gmax, gidx = first_max(gl)
    p_top = 1.0 / jnp.sum(jnp.exp(gl - gmax), axis=-1, keepdims=True)
    lo = N_GROUPS + EXPERTS_PER_GROUP * gidx
    el = jnp.where(lanef >= lo, jnp.where(lanef < lo + EXPERTS_PER_GROUP, logits, NEG_BIG), NEG_BIG)
    m1, i1 = first_max(el)
    m2, i2 = first_max(jnp.where(lanef == i1, NEG_BIG, el))
    t = jnp.exp(m2 - m1)
    w1 = p_top / (1.0 + t)
    w2 = w1 * t
    return jnp.where(lane == 0, i1 - N_GROUPS,
                     jnp.where(lane == 1, i2 - N_GROUPS,
                               jnp.where(lane == 2, w1, jnp.where(lane == 3, w2, 0.0))))


def _slab_store(ref, val):
    rows, width = val.shape
    ns = width // LANES
    for s in range(ns):
        ref[pl.ds(s, rows, stride=ns), :] = val[:, s * LANES:(s + 1) * LANES]


def _slab_load(ref, first_row, rows, ns):
    return [ref[pl.ds(first_row * ns + s, rows, stride=ns), :] for s in range(ns)]


def _outproj_kernel(yp_ref, yd_ref, wo_ref, x_ref, mod_ref, g_ref, wr_ref, br_ref,
                    x2_ref, h2_ref, route_ref):
    half = yp_ref.shape[1]
    out = _dot(yp_ref[...], wo_ref[:half, :]) + _dot(yd_ref[...], wo_ref[half:, :])
    x2 = x_ref[...] + mod_ref[0, 2:3, :] * out
    x2_ref[...] = x2
    ms = jnp.mean(x2 * x2, axis=-1, keepdims=True)
    y = x2 * lax.rsqrt(ms + EPS) * g_ref[...]
    h2 = y * (1.0 + mod_ref[0, 4:5, :]) + mod_ref[0, 3:4, :]
    _slab_store(h2_ref, h2)
    hi = h2.astype(BF16)
    lo = (h2 - hi.astype(F32)).astype(BF16)
    wr = wr_ref[...]
    whi = wr.astype(BF16)
    wlo = (wr - whi.astype(F32)).astype(BF16)
    logits = _dot(hi, whi) + (_dot(lo, whi) + _dot(hi, wlo)) + br_ref[...]
    route_ref[...] = _route(logits)


def _outproj(y_pool, y_dn, w_out_bf, x2d, mod3, norm_g, w_router, b_router, seq):
    T, D = x2d.shape
    half = y_pool.shape[1]
    tm = 256
    row = lambda i: (i, 0)
    const = lambda i: (0, 0)
    return pl.pallas_call(
        _outproj_kernel,
        grid=(T // tm,),
        in_specs=[pl.BlockSpec((tm, half), row),
                  pl.BlockSpec((tm, half), row),
                  pl.BlockSpec((2 * half, D), const),
                  pl.BlockSpec((tm, D), row),
                  pl.BlockSpec((1, 6, D), lambda i: (i * tm // seq, 0, 0)),
                  pl.BlockSpec((1, D), const),
                  pl.BlockSpec((D, LANES), const),
                  pl.BlockSpec((1, LANES), const)],
        out_specs=[pl.BlockSpec((tm, D), row),
                   pl.BlockSpec((tm * (D // LANES), LANES), row),
                   pl.BlockSpec((tm, LANES), row)],
        out_shape=[jax.ShapeDtypeStruct((T, D), F32),
                   jax.ShapeDtypeStruct((T * (D // LANES), LANES), F32),
                   jax.ShapeDtypeStruct((T, LANES), F32)],
        compiler_params=_cparams("arbitrary"),
        name="outproj",
    )(y_pool, y_dn, w_out_bf, x2d, mod3, norm_g.reshape(1, D), w_router, b_router)


GATHER_UNROLL = 8


def _row_gather(idx_ref, nrows, ns, src_hbm, dst, sem):
    def body(r, carry):
        src_row = pl.multiple_of(idx_ref[0, 0, r] * ns, ns)
        dst_row = pl.multiple_of(r * ns, ns)
        pltpu.make_async_copy(src_hbm.at[pl.ds(src_row, ns), :], dst.at[pl.ds(dst_row, ns), :],
                              sem).start()
        return carry
    lax.fori_loop(0, nrows, body, 0, unroll=GATHER_UNROLL)


def _row_gather_wait(nrows, ns, src_hbm, dst, sem):
    pltpu.make_async_copy(src_hbm.at[pl.ds(0, nrows * ns), :], dst, sem).wait()


def _expert_kernel(be_ref, nact_ref, tok_ref, tokn_ref, h_hbm, wg_ref, wu_ref, wd_ref,
                   ys_ref, xbuf, sem, *, bm, ns):
    i = pl.program_id(0)
    nact = nact_ref[0]
    slot = i % 2

    @pl.when(i == 0)
    def _():
        _row_gather(tok_ref, bm, ns, h_hbm, xbuf.at[0], sem.at[0])

    @pl.when(i + 1 < nact)
    def _():
        _row_gather(tokn_ref, bm, ns, h_hbm, xbuf.at[1 - slot], sem.at[1 - slot])

    @pl.when(i < nact)
    def _():
        _row_gather_wait(bm, ns, h_hbm, xbuf.at[slot], sem.at[slot])
        xb = jnp.concatenate(_slab_load(xbuf.at[slot], 0, bm, ns), axis=-1).astype(BF16)
        gate = _dot(xb, wg_ref[0].astype(BF16))
        up = _dot(xb, wu_ref[0].astype(BF16))
        hid = (_silu(gate) * up).astype(BF16)
        _slab_store(ys_ref, _dot(hid, wd_ref[0].astype(BF16)))

    @pl.when(i >= nact)
    def _():
        ys_ref[...] = jnp.zeros_like(ys_ref)


def _experts(h2_slab, block_e, nact, buf_tok, w_gate, w_up, w_down):
    E, D, De = w_gate.shape
    ns = D // LANES
    bm = MOE_BLOCK
    n_pad = buf_tok.shape[0]
    nb = n_pad // bm
    tok3 = buf_tok.reshape(nb, 1, bm)
    grid_spec = pltpu.PrefetchScalarGridSpec(
        num_scalar_prefetch=2,
        grid=(nb,),
        in_specs=[pl.BlockSpec((1, 1, bm), lambda i, be, na: (i, 0, 0), memory_space=pltpu.SMEM),
                  pl.BlockSpec((1, 1, bm), lambda i, be, na: (jnp.minimum(i + 1, nb - 1), 0, 0),
                               memory_space=pltpu.SMEM),
                  pl.BlockSpec(memory_space=pl.ANY),
                  pl.BlockSpec((1, D, De), lambda i, be, na: (be[i], 0, 0)),
                  pl.BlockSpec((1, D, De), lambda i, be, na: (be[i], 0, 0)),
                  pl.BlockSpec((1, De, D), lambda i, be, na: (be[i], 0, 0))],
        out_specs=pl.BlockSpec((bm * ns, LANES), lambda i, be, na: (i, 0)),
        scratch_shapes=[pltpu.VMEM((2, bm * ns, LANES), F32), pltpu.SemaphoreType.DMA((2,))],
    )
    return pl.pallas_call(
        functools.partial(_expert_kernel, bm=bm, ns=ns),
        grid_spec=grid_spec,
        out_shape=jax.ShapeDtypeStruct((n_pad * ns, LANES), F32),
        compiler_params=_cparams("arbitrary"),
        name="experts",
    )(block_e, nact, tok3, tok3, h2_slab, w_gate, w_up, w_down)


def _combine_kernel(pos_ref, posn_ref, ys_hbm, x2_ref, mod_ref, route_ref, g_ref, o_ref,
                    ybuf, sem, *, tm, ns):
    i = pl.program_id(0)
    nsteps = pl.num_programs(0)
    slot = i % 2
    nrows = TOP_K * tm

    @pl.when(i == 0)
    def _():
        _row_gather(pos_ref, nrows, ns, ys_hbm, ybuf.at[0], sem.at[0])

    @pl.when(i + 1 < nsteps)
    def _():
        _row_gather(posn_ref, nrows, ns, ys_hbm, ybuf.at[1 - slot], sem.at[1 - slot])

    _row_gather_wait(nrows, ns, ys_hbm, ybuf.at[slot], sem.at[slot])
    route = route_ref[...]
    w0 = route[:, 2:3]
    w1 = route[:, 3:4]
    y0 = _slab_load(ybuf.at[slot], 0, tm, ns)
    y1 = _slab_load(ybuf.at[slot], tm, tm, ns)
    y = jnp.concatenate([w0 * a + w1 * b for a, b in zip(y0, y1)], axis=-1)
    x3 = x2_ref[...] + mod_ref[0, 5:6, :] * y
    ms = jnp.mean(x3 * x3, axis=-1, keepdims=True)
    o_ref[...] = x3 * lax.rsqrt(ms + EPS) * g_ref[...]


def _combine(ys_slab, pos, x2, mod3, route, norm_g, seq):
    T, D = x2.shape
    ns = D // LANES
    tm = 256
    nt = T // tm
    pos3 = pos.reshape(nt, tm, TOP_K).transpose(0, 2, 1).reshape(nt, 1, TOP_K * tm)
    row = lambda i: (i, 0)
    return pl.pallas_call(
        functools.partial(_combine_kernel, tm=tm, ns=ns),
        grid=(nt,),
        in_specs=[pl.BlockSpec((1, 1, TOP_K * tm), lambda i: (i, 0, 0), memory_space=pltpu.SMEM),
                  pl.BlockSpec((1, 1, TOP_K * tm), lambda i: (jnp.minimum(i + 1, nt - 1), 0, 0),
                               memory_space=pltpu.SMEM),
                  pl.BlockSpec(memory_space=pl.ANY),
                  pl.BlockSpec((tm, D), row),
                  pl.BlockSpec((1, 6, D), lambda i: (i * tm // seq, 0, 0)),
                  pl.BlockSpec((tm, LANES), row),
                  pl.BlockSpec((1, D), lambda i: (0, 0))],
        out_specs=pl.BlockSpec((tm, D), row),
        out_shape=jax.ShapeDtypeStruct((T, D), F32),
        scratch_shapes=[pltpu.VMEM((2, TOP_K * tm * ns, LANES), F32),
                        pltpu.SemaphoreType.DMA((2,))],
        compiler_params=_cparams("arbitrary"),
        name="combine",
    )(pos3, pos3, ys_slab, x2, mod3, route, norm_g.reshape(1, D))


def _dispatch_plan(route, n_tokens):
    A = n_tokens * TOP_K
    bm = MOE_BLOCK
    nb = -(-(A + N_EXPERTS * (bm - 1)) // bm)
    flat_e = route[:, :TOP_K].astype(jnp.int32).reshape(A)
    iota = jnp.arange(A, dtype=jnp.int32)
    _, order = lax.sort_key_val(flat_e, iota)
    _, inv = lax.sort_key_val(order, iota)
    onehot = flat_e[:, None] == jnp.arange(N_EXPERTS, dtype=jnp.int32)[None, :]
    counts = jnp.sum(onehot, axis=0, dtype=jnp.int32)
    padded = (counts + bm - 1) // bm * bm
    pad_end = jnp.cumsum(padded)
    pad_start = pad_end - padded
    start = jnp.cumsum(counts) - counts
    pos = inv + jnp.sum(jnp.where(onehot, (pad_start - start)[None, :], 0), axis=1)
    block_start = jnp.arange(nb, dtype=jnp.int32) * bm
    block_e = jnp.minimum(jnp.sum(pad_end[None, :] <= block_start[:, None], axis=1),
                          N_EXPERTS - 1).astype(jnp.int32)
    r = (block_start - pad_start[block_e])[:, None] + jnp.arange(bm, dtype=jnp.int32)[None, :]
    src = jnp.clip(start[block_e][:, None] + r, 0, A - 1)
    buf_tok = jnp.where(r < counts[block_e][:, None], order[src] // TOP_K, 0).reshape(nb * bm)
    nact = (pad_end[-1:] // bm).astype(jnp.int32)
    return buf_tok, pos.astype(jnp.int32), block_e, nact


def kernel(x, c, w_ada, b_ada, norm1_g, w_in, pool_w, pool_scale, conv_w, a_log, dt_bias,
           o_norm_g, w_out, norm2_g, w_router_group, b_router_group, w_router_expert,
           b_router_expert, w_gate, w_up, w_down, norm_f_g):
    B, S, D = x.shape
    T = B * S
    depth = w_ada.shape[0]
    pool_width = pool_w.shape[1] * pool_w.shape[2]
    n_main = pool_width + 4 * DN_HEADS * DN_HEAD_DIM
    n_route = N_GROUPS + N_EXPERTS

    assert depth == 1, "kernel supports the single-layer configuration only"
    l = 0
    xt = x.reshape(T, D)
    mod3 = _ada(c, w_ada[l], b_ada[l]).reshape(B, 6, D)

    w_main = w_in[l, :, :n_main].astype(BF16)
    w_ba = jnp.pad(w_in[l, :, n_main:], ((0, 0), (0, LANES - 2 * DN_HEADS))).astype(BF16)
    proj, ba = _inproj(xt, mod3, norm1_g[l], w_main, w_ba, S)

    y_pool = _pool(proj, pool_w[l], pool_scale[l], B, S)
    y_dn = _deltanet(proj, ba, conv_w[l], a_log[l], dt_bias[l], o_norm_g[l], B, S, pool_width)

    w_router = jnp.pad(jnp.concatenate([w_router_group[l], w_router_expert[l]], axis=1),
                       ((0, 0), (0, LANES - n_route)))
    b_router = jnp.pad(jnp.concatenate([b_router_group[l], b_router_expert[l]]),
                       (0, LANES - n_route)).reshape(1, LANES)
    x2, h2, route = _outproj(y_pool, y_dn, w_out[l].astype(BF16), xt, mod3, norm2_g[l],
                             w_router, b_router, S)

    buf_tok, pos, block_e, nact = _dispatch_plan(route, T)
    ys = _experts(h2, block_e, nact, buf_tok, w_gate[l], w_up[l], w_down[l])
    out = _combine(ys, pos, x2, mod3, route, norm_f_g, S)
    return out.reshape(B, S, D)
```

```python
import functools

import jax
import jax.numpy as jnp
from jax import lax
from jax.experimental import pallas as pl
from jax.experimental.pallas import tpu as pltpu

F32 = jnp.float32
BF16 = jnp.bfloat16

POOL_GROUPS = 4
POOL_WINDOWS = (2, 4, 8, 16)
POOL_HALO = 16
DN_HEADS = 8
DN_HEAD_DIM = 128
CONV_WIDTH = 4
DN_CHUNK = 128
N_GROUPS = 4
EXPERTS_PER_GROUP = 8
N_EXPERTS = N_GROUPS * EXPERTS_PER_GROUP
TOP_K = 2
MOE_BLOCK = 256
EPS = 1e-6
LANES = 128
NEG_BIG = -3.0e38
VMEM_LIMIT = 60 * 1024 * 1024


def _silu(x):
    half = 0.5 * x
    return half * (1.0 + jnp.tanh(half))


def _chunk_cumsum(x, chunk):
    rmod = lax.broadcasted_iota(jnp.int32, x.shape, 0) & (chunk - 1)
    k = 1
    while k < chunk:
        x = x + jnp.where(rmod >= k, pltpu.roll(x, k, axis=0), 0.0)
        k *= 2
    return x


def _dot(a, b):
    return jnp.dot(a, b, preferred_element_type=F32)


def _cparams(*sem):
    return pltpu.CompilerParams(dimension_semantics=sem, vmem_limit_bytes=VMEM_LIMIT)


def _ada_kernel(c_ref, w_ref, b_ref, o_ref):
    ca = _silu(c_ref[...])
    o_ref[...] = jnp.dot(ca, w_ref[...], preferred_element_type=F32,
                         precision=lax.Precision.HIGHEST) + b_ref[...]


def _ada(c, w_ada, b_ada):
    B, D = c.shape
    N = w_ada.shape[1]
    tn = 1024
    return pl.pallas_call(
        _ada_kernel,
        grid=(N // tn,),
        in_specs=[pl.BlockSpec((B, D), lambda j: (0, 0)),
                  pl.BlockSpec((D, tn), lambda j: (0, j)),
                  pl.BlockSpec((1, tn), lambda j: (0, j))],
        out_specs=pl.BlockSpec((B, tn), lambda j: (0, j)),
        out_shape=jax.ShapeDtypeStruct((B, N), F32),
        compiler_params=_cparams("arbitrary"),
        name="ada",
    )(c, w_ada, b_ada.reshape(1, N))


def _inproj_kernel(x_ref, mod_ref, g_ref, w_ref, wba_ref, alog_ref, dtb_ref,
                   proj_ref, gates_ref, h_ref):
    @pl.when(pl.program_id(1) == 0)
    def _():
        x = x_ref[...]
        ms = jnp.mean(x * x, axis=-1, keepdims=True)
        y = x * lax.rsqrt(ms + EPS) * g_ref[...]
        h = (y * (1.0 + mod_ref[0, 1:2, :]) + mod_ref[0, 0:1, :]).astype(BF16)
        h_ref[...] = h
        ba = _dot(h, wba_ref[...])
        lane = lax.broadcasted_iota(jnp.int32, ba.shape, 1)
        a = ba + dtb_ref[...]
        softplus = jnp.maximum(a, 0.0) + jnp.log1p(jnp.exp(-jnp.abs(a)))
        gc = _chunk_cumsum(-jnp.exp(alog_ref[...]) * softplus, DN_CHUNK)
        gates_ref[...] = jnp.where(lane < DN_HEADS, jax.nn.sigmoid(ba), gc)

    proj_ref[...] = _dot(h_ref[...], w_ref[...])


def _inproj(x2d, mod3, norm_g, w_main, w_ba, alog_lanes, dtb_lanes, seq):
    T, D = x2d.shape
    N = w_main.shape[1]
    tm, tn = 1024, 1024
    assert tm % DN_CHUNK == 0 and seq % tm == 0
    return pl.pallas_call(
        _inproj_kernel,
        grid=(T // tm, N // tn),
        in_specs=[pl.BlockSpec((tm, D), lambda i, j: (i, 0)),
                  pl.BlockSpec((1, 6, D), lambda i, j: (i * tm // seq, 0, 0)),
                  pl.BlockSpec((1, D), lambda i, j: (0, 0)),
                  pl.BlockSpec((D, tn), lambda i, j: (0, j)),
                  pl.BlockSpec((D, LANES), lambda i, j: (0, 0)),
                  pl.BlockSpec((1, LANES), lambda i, j: (0, 0)),
                  pl.BlockSpec((1, LANES), lambda i, j: (0, 0))],
        out_specs=[pl.BlockSpec((tm, tn), lambda i, j: (i, j)),
                   pl.BlockSpec((tm, LANES), lambda i, j: (i, 0))],
        out_shape=[jax.ShapeDtypeStruct((T, N), F32),
                   jax.ShapeDtypeStruct((T, LANES), F32)],
        scratch_shapes=[pltpu.VMEM((tm, D), BF16)],
        compiler_params=_cparams("arbitrary", "arbitrary"),
        name="inproj",
    )(x2d, mod3, norm_g.reshape(1, D), w_main, w_ba, alog_lanes, dtb_lanes)


def _pool_kernel(u_ref, halo_ref, w_ref, sc_ref, o_ref, *, ts, gd):
    i = pl.program_id(1)
    halo = jnp.where(i > 0, halo_ref[...], 0.0)
    tpos = lax.broadcasted_iota(jnp.int32, (ts, 1), 0) + i * ts + 1
    for g in range(POOL_GROUPS):
        win = POOL_WINDOWS[g]
        u = u_ref[:, g * gd:(g + 1) * gd]
        s = jnp.concatenate([halo[:, g * gd:(g + 1) * gd], u], axis=0)
        k = 1
        while k < win:
            s = s + pltpu.roll(s, k, axis=0)
            k *= 2
        cnt = jnp.minimum(tpos, win).astype(F32)
        diff = s[POOL_HALO:, :] / cnt - u
        y = _dot(diff.astype(BF16), w_ref[g].astype(BF16)) * sc_ref[:, g * gd:(g + 1) * gd]
        o_ref[:, g * gd:(g + 1) * gd] = y.astype(o_ref.dtype)


def _pool(proj, pool_w, pool_scale, batch, seq):
    T = proj.shape[0]
    G, gd, _ = pool_w.shape
    width = G * gd
    ts = 512
    nt = seq // ts
    hb = ts // POOL_HALO
    return pl.pallas_call(
        functools.partial(_pool_kernel, ts=ts, gd=gd),
        grid=(batch, nt),
        in_specs=[pl.BlockSpec((ts, width), lambda b, i: (b * nt + i, 0)),
                  pl.BlockSpec((POOL_HALO, width),
                               lambda b, i: (jnp.maximum((b * nt + i) * hb - 1, 0), 0)),
                  pl.BlockSpec((G, gd, gd), lambda b, i: (0, 0, 0)),
                  pl.BlockSpec((1, width), lambda b, i: (0, 0))],
        out_specs=pl.BlockSpec((ts, width), lambda b, i: (b * nt + i, 0)),
        out_shape=jax.ShapeDtypeStruct((T, width), BF16),
        compiler_params=_cparams("arbitrary", "arbitrary"),
        name="pool",
    )(proj, proj, pool_w, pool_scale.reshape(1, width))


def _bmm(a, b):
    return jnp.einsum('nij,njk->nik', a.astype(BF16), b.astype(BF16),
                      preferred_element_type=F32)


def _bmm_nt(a, b):
    return jnp.einsum('nid,njd->nij', a.astype(BF16), b.astype(BF16),
                      preferred_element_type=F32)


def _dn_prepare(q_raw, k_raw, v_raw, cwq, cwk, cwv, beta_col, gc_col, *, chunk):
    seq, d = q_raw.shape
    n = seq // chunk
    top = lax.broadcasted_iota(jnp.int32, (8, d), 0)

    def conv_silu(x, cw):
        acc = x * cw[CONV_WIDTH - 1:CONV_WIDTH, :]
        for j in range(CONV_WIDTH - 1):
            sh = CONV_WIDTH - 1 - j
            r = pltpu.roll(x, sh, axis=0)
            r = jnp.concatenate([jnp.where(top >= sh, r[:8], 0.0), r[8:]], axis=0)
            acc = acc + r * cw[j:j + 1, :]
        return _silu(acc)

    def l2n(x):
        return x * lax.rsqrt(jnp.sum(x * x, axis=-1, keepdims=True) + EPS)

    qn = l2n(conv_silu(q_raw, cwq)) * (d ** -0.5)
    kn = l2n(conv_silu(k_raw, cwk))
    v = conv_silu(v_raw, cwv)
    beta = jnp.broadcast_to(beta_col, (seq, d))
    gc = jnp.broadcast_to(gc_col, (seq, d))

    gc3 = gc.reshape(n, chunk, d)
    ii = lax.broadcasted_iota(jnp.int32, (chunk, chunk), 0)
    jj = lax.broadcasted_iota(jnp.int32, (chunk, chunk), 1)
    incl = (ii >= jj)[None]
    strict = (ii > jj)[None]
    diff = gc3 - jnp.swapaxes(gc3, 1, 2)
    decay = jnp.where(incl, jnp.exp(jnp.where(incl, diff, 0.0)), 0.0)

    kb = kn * beta
    kn3 = kn.reshape(n, chunk, d)
    lmat = jnp.where(strict, _bmm_nt(kb.reshape(n, chunk, d), kn3) * decay, 0.0)
    qk = jnp.where(incl, _bmm_nt(qn.reshape(n, chunk, d), kn3) * decay, 0.0)

    def merge_mask(lv):
        same = (ii >> (lv + 1)) == (jj >> (lv + 1))
        return (same & (((ii >> lv) & 1) == 1) & (((jj >> lv) & 1) == 0))[None]

    tinv = (ii == jj).astype(F32)[None] - jnp.where(merge_mask(0), lmat, 0.0)
    for lv in range(1, chunk.bit_length() - 1):
        a21 = jnp.where(merge_mask(lv), lmat, 0.0)
        tinv = tinv - _bmm(tinv, _bmm(a21, tinv))

    egc = jnp.exp(gc)
    rhs = jnp.concatenate([v * beta, kb * egc], axis=-1).reshape(n, chunk, 2 * d)
    uw = _bmm(tinv, rhs)
    glast = gc3[:, chunk - 1:chunk, :]
    kdt = jnp.swapaxes(kn3 * jnp.exp(glast - gc3), 1, 2)
    kuw = _bmm(kdt, uw)
    quw = _bmm(qk, uw)
    qp = (qn * egc).reshape(n, chunk, d) - quw[..., d:]
    return quw[..., :d], qp, kuw[..., :d], kuw[..., d:], jnp.exp(glast)


def _dn_kernel(q_ref, k_ref, v_ref, z_ref, gates_ref, cwq_ref, cwk_ref, cwv_ref, on_ref,
               o_ref, o_scr, *, chunk, heads_per_step):
    d = DN_HEAD_DIM
    seq = q_ref.shape[0]
    n = seq // chunk
    lane = lax.broadcasted_iota(jnp.int32, (seq, LANES), 1)
    gates = gates_ref[...]
    prepared = []
    for i in range(heads_per_step):
        head = pl.program_id(1) * heads_per_step + i
        cols = slice(i * d, (i + 1) * d)
        beta_col = jnp.sum(jnp.where(lane == head, gates, 0.0), axis=-1, keepdims=True)
        gc_col = jnp.sum(jnp.where(lane == head + DN_HEADS, gates, 0.0), axis=-1, keepdims=True)
        prepared.append(_dn_prepare(q_ref[:, cols], k_ref[:, cols], v_ref[:, cols],
                                    cwq_ref[:, cols], cwk_ref[:, cols], cwv_ref[:, cols],
                                    beta_col, gc_col, chunk=chunk))

    states = [jnp.zeros((d, d), F32) for _ in range(heads_per_step)]
    for c in range(n):
        for i, (o0, qp, kub, kuw, eglast) in enumerate(prepared):
            sb = states[i].astype(BF16)
            o_scr[c * chunk:(c + 1) * chunk, i * d:(i + 1) * d] = o0[c] + _dot(qp[c].astype(BF16), sb)
            states[i] = states[i] * eglast[c] + kub[c] - _dot(kuw[c].astype(BF16), sb)

    for i in range(heads_per_step):
        cols = slice(i * d, (i + 1) * d)
        o = o_scr[:, cols]
        o = o * lax.rsqrt(jnp.mean(o * o, axis=-1, keepdims=True) + EPS) * on_ref[...]
        o_ref[:, cols] = (o * _silu(z_ref[:, cols])).astype(o_ref.dtype)


def _deltanet(proj, gates, conv_w, o_norm_g, batch, seq, col0):
    T = proj.shape[0]
    d = DN_HEAD_DIM
    H = DN_HEADS
    hps = 2
    w = hps * d
    cb = col0 // w
    nhb = H // hps

    def colspec(off):
        return pl.BlockSpec((seq, w), lambda b, h: (b, off + h))

    def convspec(off):
        return pl.BlockSpec((CONV_WIDTH, w), lambda b, h: (0, off + h))

    return pl.pallas_call(
        functools.partial(_dn_kernel, chunk=DN_CHUNK, heads_per_step=hps),
        grid=(batch, nhb),
        in_specs=[colspec(cb), colspec(cb + nhb), colspec(cb + 2 * nhb), colspec(cb + 3 * nhb),
                  pl.BlockSpec((seq, LANES), lambda b, h: (b, 0)),
                  convspec(0), convspec(nhb), convspec(2 * nhb),
                  pl.BlockSpec((1, d), lambda b, h: (0, 0))],
        out_specs=pl.BlockSpec((seq, w), lambda b, h: (b, h)),
        out_shape=jax.ShapeDtypeStruct((T, H * d), BF16),
        scratch_shapes=[pltpu.VMEM((seq, w), F32)],
        compiler_params=_cparams("arbitrary", "arbitrary"),
        name="deltanet",
    )(proj, proj, proj, proj, gates, conv_w, conv_w, conv_w, o_norm_g.reshape(1, d))


def _route(logits):
    lane = lax.broadcasted_iota(jnp.int32, logits.shape, 1)
    lanef = lane.astype(F32)
    far = float(LANES)

    def first_max(vals):
        m = jnp.max(vals, axis=-1, keepdims=True)
        idx = jnp.min(jnp.where(vals == m, lanef, far), axis=-1, keepdims=True)
        return m, idx

    gl = jnp.where(lane < N_GROUPS, logits, NEG_BIG)
    gmax, gidx = first_max(gl)
    p_top = 1.0 / jnp.sum(jnp.exp(gl - gmax), axis=-1, keepdims=True)
    lo = N_GROUPS + EXPERTS_PER_GROUP * gidx
    el = jnp.where(lanef >= lo, jnp.where(lanef < lo + EXPERTS_PER_GROUP, logits, NEG_BIG), NEG_BIG)
    m1, i1 = first_max(el)
    m2, i2 = first_max(jnp.where(lanef == i1, NEG_BIG, el))
    t = jnp.exp(m2 - m1)
    w1 = p_top / (1.0 + t)
    w2 = w1 * t
    return jnp.where(lane == 0, i1 - N_GROUPS,
                     jnp.where(lane == 1, i2 - N_GROUPS,
                               jnp.where(lane == 2, w1, jnp.where(lane == 3, w2, 0.0))))


def _slab_store(ref, val):
    rows, width = val.shape
    ns = width // LANES
    for s in range(ns):
        ref[pl.ds(s, rows, stride=ns), :] = val[:, s * LANES:(s + 1) * LANES]


def _slab_load(ref, first_row, rows, ns):
    return [ref[pl.ds(first_row * ns + s, rows, stride=ns), :] for s in range(ns)]


def _outproj_kernel(yp_ref, yd_ref, wo_ref, x_ref, mod_ref, g_ref, wr_ref, br_ref,
                    x2_ref, h2_ref, route_ref):
    half = yp_ref.shape[1]
    out = _dot(yp_ref[...], wo_ref[:half, :]) + _dot(yd_ref[...], wo_ref[half:, :])
    x2 = x_ref[...] + mod_ref[0, 2:3, :] * out
    x2_ref[...] = x2
    ms = jnp.mean(x2 * x2, axis=-1, keepdims=True)
    y = x2 * lax.rsqrt(ms + EPS) * g_ref[...]
    h2 = y * (1.0 + mod_ref[0, 4:5, :]) + mod_ref[0, 3:4, :]
    _slab_store(h2_ref, h2)
    hi = h2.astype(BF16)
    lo = (h2 - hi.astype(F32)).astype(BF16)
    whi = wr_ref[0]
    logits = _dot(hi, whi) + (_dot(lo, whi) + _dot(hi, wr_ref[1])) + br_ref[...]
    route_ref[...] = _route(logits)


def _outproj(y_pool, y_dn, w_out_bf, x2d, mod3, norm_g, w_router, b_router, seq):
    T, D = x2d.shape
    half = y_pool.shape[1]
    tm = 256
    row = lambda i: (i, 0)
    const = lambda i: (0, 0)
    return pl.pallas_call(
        _outproj_kernel,
        grid=(T // tm,),
        in_specs=[pl.BlockSpec((tm, half), row),
                  pl.BlockSpec((tm, half), row),
                  pl.BlockSpec((2 * half, D), const),
                  pl.BlockSpec((tm, D), row),
                  pl.BlockSpec((1, 6, D), lambda i: (i * tm // seq, 0, 0)),
                  pl.BlockSpec((1, D), const),
                  pl.BlockSpec((2, D, LANES), lambda i: (0, 0, 0)),
                  pl.BlockSpec((1, LANES), const)],
        out_specs=[pl.BlockSpec((tm, D), row),
                   pl.BlockSpec((tm * (D // LANES), LANES), row),
                   pl.BlockSpec((tm, LANES), row)],
        out_shape=[jax.ShapeDtypeStruct((T, D), F32),
                   jax.ShapeDtypeStruct((T * (D // LANES), LANES), F32),
                   jax.ShapeDtypeStruct((T, LANES), F32)],
        compiler_params=_cparams("arbitrary"),
        name="outproj",
    )(y_pool, y_dn, w_out_bf, x2d, mod3, norm_g.reshape(1, D), w_router, b_router)


GATHER_UNROLL = 8


def _row_gather(idx_ref, nrows, ns, src_hbm, dst, sem):
    def body(r, carry):
        src_row = pl.multiple_of(idx_ref[0, 0, r] * ns, ns)
        dst_row = pl.multiple_of(r * ns, ns)
        pltpu.make_async_copy(src_hbm.at[pl.ds(src_row, ns), :], dst.at[pl.ds(dst_row, ns), :],
                              sem).start()
        return carry
    lax.fori_loop(0, nrows, body, 0, unroll=GATHER_UNROLL)


def _row_gather_wait(nrows, ns, src_hbm, dst, sem):
    pltpu.make_async_copy(src_hbm.at[pl.ds(0, nrows * ns), :], dst, sem).wait()


WEIGHT_DMA_PRIORITY = 1


def _expert_kernel(be_ref, first_ref, wslot_ref, nexte_ref, nact_ref, tok_ref, tokn_ref,
                   h_hbm, wg_hbm, wu_hbm, wd_hbm, ys_ref, xbuf, wg_buf, wu_buf, wd_buf,
                   sem, wsem, *, bm, ns):
    i = pl.program_id(0)
    nact = nact_ref[0]
    slot = i % 2
    wslot = wslot_ref[i]

    def weight_copies(e, s):
        return (pltpu.make_async_copy(wg_hbm.at[e], wg_buf.at[s], wsem.at[s]),
                pltpu.make_async_copy(wu_hbm.at[e], wu_buf.at[s], wsem.at[s]),
                pltpu.make_async_copy(wd_hbm.at[e], wd_buf.at[s], wsem.at[s]))

    @pl.when(i == 0)
    def _():
        for cp in weight_copies(be_ref[0], 0):
            cp.start(priority=WEIGHT_DMA_PRIORITY)
        _row_gather(tok_ref, bm, ns, h_hbm, xbuf.at[0], sem.at[0])

    @pl.when((first_ref[i] == 1) & (nexte_ref[i] >= 0))
    def _():
        for cp in weight_copies(nexte_ref[i], 1 - wslot):
            cp.start(priority=WEIGHT_DMA_PRIORITY)

    @pl.when(i + 1 < nact)
    def _():
        _row_gather(tokn_ref, bm, ns, h_hbm, xbuf.at[1 - slot], sem.at[1 - slot])

    @pl.when(first_ref[i] == 1)
    def _():
        for cp in weight_copies(be_ref[i], wslot):
            cp.wait()

    @pl.when(i < nact)
    def _():
        _row_gather_wait(bm, ns, h_hbm, xbuf.at[slot], sem.at[slot])
        xb = jnp.concatenate(_slab_load(xbuf.at[slot], 0, bm, ns), axis=-1).astype(BF16)
        gate = _dot(xb, wg_buf[wslot].astype(BF16))
        up = _dot(xb, wu_buf[wslot].astype(BF16))
        hid = (_silu(gate) * up).astype(BF16)
        _slab_store(ys_ref, _dot(hid, wd_buf[wslot].astype(BF16)))

    @pl.when(i >= nact)
    def _():
        ys_ref[...] = jnp.zeros_like(ys_ref)


def _experts(h2_slab, block_e, nact, buf_tok, w_gate, w_up, w_down):
    E, D, De = w_gate.shape
    ns = D // LANES
    bm = MOE_BLOCK
    n_pad = buf_tok.shape[0]
    nb = n_pad // bm
    tok3 = buf_tok.reshape(nb, 1, bm)

    idx = jnp.arange(nb, dtype=jnp.int32)
    active = idx < nact[0]
    prev_e = jnp.concatenate([jnp.full((1,), -1, jnp.int32), block_e[:-1]])
    first = (active & (block_e != prev_e)).astype(jnp.int32)
    wslot = ((jnp.cumsum(first) - 1) % 2).astype(jnp.int32)
    later_first = (first[None, :] == 1) & (idx[None, :] > idx[:, None])
    nxt = jnp.min(jnp.where(later_first, idx[None, :], nb), axis=1)
    next_e = jnp.where(nxt < nb, block_e[jnp.minimum(nxt, nb - 1)], -1).astype(jnp.int32)

    any_spec = pl.BlockSpec(memory_space=pl.ANY)
    grid_spec = pltpu.PrefetchScalarGridSpec(
        num_scalar_prefetch=5,
        grid=(nb,),
        in_specs=[pl.BlockSpec((1, 1, bm), lambda i, *_: (i, 0, 0), memory_space=pltpu.SMEM),
                  pl.BlockSpec((1, 1, bm), lambda i, *_: (jnp.minimum(i + 1, nb - 1), 0, 0),
                               memory_space=pltpu.SMEM),
                  any_spec, any_spec, any_spec, any_spec],
        out_specs=pl.BlockSpec((bm * ns, LANES), lambda i, *_: (i, 0)),
        scratch_shapes=[pltpu.VMEM((2, bm * ns, LANES), F32),
                        pltpu.VMEM((2, D, De), F32), pltpu.VMEM((2, D, De), F32),
                        pltpu.VMEM((2, De, D), F32),
                        pltpu.SemaphoreType.DMA((2,)), pltpu.SemaphoreType.DMA((2,))],
    )
    return pl.pallas_call(
        functools.partial(_expert_kernel, bm=bm, ns=ns),
        grid_spec=grid_spec,
        out_shape=jax.ShapeDtypeStruct((n_pad * ns, LANES), F32),
        compiler_params=_cparams("arbitrary"),
        name="experts",
    )(block_e, first, wslot, next_e, nact, tok3, tok3, h2_slab, w_gate, w_up, w_down)


def _combine_kernel(pos_ref, posn_ref, ys_hbm, x2_ref, mod_ref, route_ref, g_ref, o_ref,
                    ybuf, sem, *, tm, ns):
    i = pl.program_id(0)
    nsteps = pl.num_programs(0)
    slot = i % 2
    nrows = TOP_K * tm

    @pl.when(i == 0)
    def _():
        _row_gather(pos_ref, nrows, ns, ys_hbm, ybuf.at[0], sem.at[0])

    @pl.when(i + 1 < nsteps)
    def _():
        _row_gather(posn_ref, nrows, ns, ys_hbm, ybuf.at[1 - slot], sem.at[1 - slot])

    _row_gather_wait(nrows, ns, ys_hbm, ybuf.at[slot], sem.at[slot])
    route = route_ref[...]
    w0 = route[:, 2:3]
    w1 = route[:, 3:4]
    y0 = _slab_load(ybuf.at[slot], 0, tm, ns)
    y1 = _slab_load(ybuf.at[slot], tm, tm, ns)
    y = jnp.concatenate([w0 * a + w1 * b for a, b in zip(y0, y1)], axis=-1)
    x3 = x2_ref[...] + mod_ref[0, 5:6, :] * y
    ms = jnp.mean(x3 * x3, axis=-1, keepdims=True)
    o_ref[...] = x3 * lax.rsqrt(ms + EPS) * g_ref[...]


def _combine(ys_slab, pos, x2, mod3, route, norm_g, seq):
    T, D = x2.shape
    ns = D // LANES
    tm = 256
    nt = T // tm
    pos3 = pos.reshape(nt, tm, TOP_K).transpose(0, 2, 1).reshape(nt, 1, TOP_K * tm)
    row = lambda i: (i, 0)
    return pl.pallas_call(
        functools.partial(_combine_kernel, tm=tm, ns=ns),
        grid=(nt,),
        in_specs=[pl.BlockSpec((1, 1, TOP_K * tm), lambda i: (i, 0, 0), memory_space=pltpu.SMEM),
                  pl.BlockSpec((1, 1, TOP_K * tm), lambda i: (jnp.minimum(i + 1, nt - 1), 0, 0),
                               memory_space=pltpu.SMEM),
                  pl.BlockSpec(memory_space=pl.ANY),
                  pl.BlockSpec((tm, D), row),
                  pl.BlockSpec((1, 6, D), lambda i: (i * tm // seq, 0, 0)),
                  pl.BlockSpec((tm, LANES), row),
                  pl.BlockSpec((1, D), lambda i: (0, 0))],
        out_specs=pl.BlockSpec((tm, D), row),
        out_shape=jax.ShapeDtypeStruct((T, D), F32),
        scratch_shapes=[pltpu.VMEM((2, TOP_K * tm * ns, LANES), F32),
                        pltpu.SemaphoreType.DMA((2,))],
        compiler_params=_cparams("arbitrary"),
        name="combine",
    )(pos3, pos3, ys_slab, x2, mod3, route, norm_g.reshape(1, D))


def _dispatch_plan(route, n_tokens):
    A = n_tokens * TOP_K
    bm = MOE_BLOCK
    nb = -(-(A + N_EXPERTS * (bm - 1)) // bm)
    flat_e = route[:, :TOP_K].astype(jnp.int32).reshape(A)
    iota = jnp.arange(A, dtype=jnp.int32)
    _, order = lax.sort_key_val(flat_e, iota)
    _, inv = lax.sort_key_val(order, iota)
    onehot = flat_e[:, None] == jnp.arange(N_EXPERTS, dtype=jnp.int32)[None, :]
    counts = jnp.sum(onehot, axis=0, dtype=jnp.int32)
    padded = (counts + bm - 1) // bm * bm
    pad_end = jnp.cumsum(padded)
    pad_start = pad_end - padded
    start = jnp.cumsum(counts) - counts
    pos = inv + jnp.sum(jnp.where(onehot, (pad_start - start)[None, :], 0), axis=1)
    block_start = jnp.arange(nb, dtype=jnp.int32) * bm
    block_e = jnp.minimum(jnp.sum(pad_end[None, :] <= block_start[:, None], axis=1),
                          N_EXPERTS - 1).astype(jnp.int32)
    r = (block_start - pad_start[block_e])[:, None] + jnp.arange(bm, dtype=jnp.int32)[None, :]
    src = jnp.clip(start[block_e][:, None] + r, 0, A - 1)
    buf_tok = jnp.where(r < counts[block_e][:, None], order[src] // TOP_K, 0).reshape(nb * bm)
    nact = (pad_end[-1:] // bm).astype(jnp.int32)
    return buf_tok, pos.astype(jnp.int32), block_e, nact


def kernel(x, c, w_ada, b_ada, norm1_g, w_in, pool_w, pool_scale, conv_w, a_log, dt_bias,
           o_norm_g, w_out, norm2_g, w_router_group, b_router_group, w_router_expert,
           b_router_expert, w_gate, w_up, w_down, norm_f_g):
    B, S, D = x.shape
    T = B * S
    depth = w_ada.shape[0]
    pool_width = pool_w.shape[1] * pool_w.shape[2]
    n_main = pool_width + 4 * DN_HEADS * DN_HEAD_DIM
    n_route = N_GROUPS + N_EXPERTS

    assert depth == 1, "kernel supports the single-layer configuration only"
    l = 0
    xt = x.reshape(T, D)
    mod3 = _ada(c, w_ada[l], b_ada[l]).reshape(B, 6, D)

    w_main = w_in[l, :, :n_main].astype(BF16)
    w_ba = jnp.pad(w_in[l, :, n_main:], ((0, 0), (0, LANES - 2 * DN_HEADS))).astype(BF16)
    gate_pad = (DN_HEADS, LANES - 2 * DN_HEADS)
    alog_lanes = jnp.pad(a_log[l], gate_pad).reshape(1, LANES)
    dtb_lanes = jnp.pad(dt_bias[l], gate_pad).reshape(1, LANES)
    proj, gates = _inproj(xt, mod3, norm1_g[l], w_main, w_ba, alog_lanes, dtb_lanes, S)

    y_pool = _pool(proj, pool_w[l], pool_scale[l], B, S)
    y_dn = _deltanet(proj, gates, conv_w[l], o_norm_g[l], B, S, pool_width)

    w_router = jnp.pad(jnp.concatenate([w_router_group[l], w_router_expert[l]], axis=1),
                       ((0, 0), (0, LANES - n_route)))
    b_router = jnp.pad(jnp.concatenate([b_router_group[l], b_router_expert[l]]),
                       (0, LANES - n_route)).reshape(1, LANES)
    w_router_hi = w_router.astype(BF16)
    w_router_lo = (w_router - w_router_hi.astype(F32)).astype(BF16)
    x2, h2, route = _outproj(y_pool, y_dn, w_out[l].astype(BF16), xt, mod3, norm2_g[l],
                             jnp.stack([w_router_hi, w_router_lo]), b_router, S)

    buf_tok, pos, block_e, nact = _dispatch_plan(route, T)
    ys = _experts(h2, block_e, nact, buf_tok, w_gate[l], w_up[l], w_down[l])
    out = _combine(ys, pos, x2, mod3, route, norm_f_g, S)
    return out.reshape(B, S, D)
```

```python
import functools

import jax
import jax.numpy as jnp
from jax import lax
from jax.experimental import pallas as pl
from jax.experimental.pallas import tpu as pltpu

F32 = jnp.float32
BF16 = jnp.bfloat16

POOL_GROUPS = 4
POOL_WINDOWS = (2, 4, 8, 16)
POOL_HALO = 16
DN_HEADS = 8
DN_HEAD_DIM = 128
CONV_WIDTH = 4
DN_CHUNK = 128
N_GROUPS = 4
EXPERTS_PER_GROUP = 8
N_EXPERTS = N_GROUPS * EXPERTS_PER_GROUP
TOP_K = 2
MOE_BLOCK = 256
EPS = 1e-6
LANES = 128
MXU_COLS = 256
NEG_BIG = -3.0e38
VMEM_LIMIT = 60 * 1024 * 1024


def _silu(x):
    half = 0.5 * x
    return half * (1.0 + jnp.tanh(half))


def _chunk_cumsum(x, chunk):
    rmod = lax.broadcasted_iota(jnp.int32, x.shape, 0) & (chunk - 1)
    k = 1
    while k < chunk:
        x = x + jnp.where(rmod >= k, pltpu.roll(x, k, axis=0), 0.0)
        k *= 2
    return x


def _dot(a, b):
    return jnp.dot(a, b, preferred_element_type=F32)


def _cparams(*sem):
    return pltpu.CompilerParams(dimension_semantics=sem, vmem_limit_bytes=VMEM_LIMIT)


def _ada_kernel(c_ref, w_ref, b_ref, o_ref):
    ca = _silu(c_ref[...])
    o_ref[...] = jnp.dot(ca, w_ref[...], preferred_element_type=F32,
                         precision=lax.Precision.HIGHEST) + b_ref[...]


def _ada(c, w_ada, b_ada):
    B, D = c.shape
    N = w_ada.shape[1]
    tn = 1024
    return pl.pallas_call(
        _ada_kernel,
        grid=(N // tn,),
        in_specs=[pl.BlockSpec((B, D), lambda j: (0, 0)),
                  pl.BlockSpec((D, tn), lambda j: (0, j)),
                  pl.BlockSpec((1, tn), lambda j: (0, j))],
        out_specs=pl.BlockSpec((B, tn), lambda j: (0, j)),
        out_shape=jax.ShapeDtypeStruct((B, N), F32),
        compiler_params=_cparams("arbitrary"),
        name="ada",
    )(c, w_ada, b_ada.reshape(1, N))


def _inproj_kernel(x_ref, mod_ref, g_ref, w_ref, wba_ref, alog_ref, dtb_ref,
                   proj_ref, gates_ref, h_ref):
    @pl.when(pl.program_id(1) == 0)
    def _():
        x = x_ref[...]
        ms = jnp.mean(x * x, axis=-1, keepdims=True)
        y = x * lax.rsqrt(ms + EPS) * g_ref[...]
        h = (y * (1.0 + mod_ref[0, 1:2, :]) + mod_ref[0, 0:1, :]).astype(BF16)
        h_ref[...] = h
        proj_ref[...] = _dot(h, w_ref[...])
        ba = _dot(h, wba_ref[...])
        lane = lax.broadcasted_iota(jnp.int32, ba.shape, 1)
        a = ba + dtb_ref[...]
        softplus = jnp.maximum(a, 0.0) + jnp.log1p(jnp.exp(-jnp.abs(a)))
        gc = _chunk_cumsum(-jnp.exp(alog_ref[...]) * softplus, DN_CHUNK)
        gates_ref[...] = jnp.where(lane < DN_HEADS, jax.nn.sigmoid(ba), gc)

    @pl.when(pl.program_id(1) != 0)
    def _():
        proj_ref[...] = _dot(h_ref[...], w_ref[...])


def _inproj(x2d, mod3, norm_g, w_all, n_main, w_ba, alog_lanes, dtb_lanes, seq):
    T, D = x2d.shape
    N = n_main
    tm, tn = 1024, 1024
    assert tm % DN_CHUNK == 0 and seq % tm == 0 and N % tn == 0
    return pl.pallas_call(
        _inproj_kernel,
        grid=(T // tm, N // tn),
        in_specs=[pl.BlockSpec((tm, D), lambda i, j: (i, 0)),
                  pl.BlockSpec((1, 6, D), lambda i, j: (i * tm // seq, 0, 0)),
                  pl.BlockSpec((1, D), lambda i, j: (0, 0)),
                  pl.BlockSpec((D, tn), lambda i, j: (0, j)),
                  pl.BlockSpec((D, LANES), lambda i, j: (0, 0)),
                  pl.BlockSpec((1, LANES), lambda i, j: (0, 0)),
                  pl.BlockSpec((1, LANES), lambda i, j: (0, 0))],
        out_specs=[pl.BlockSpec((tm, tn), lambda i, j: (i, j)),
                   pl.BlockSpec((tm, LANES), lambda i, j: (i, 0))],
        out_shape=[jax.ShapeDtypeStruct((T, N), F32),
                   jax.ShapeDtypeStruct((T, LANES), F32)],
        scratch_shapes=[pltpu.VMEM((tm, D), BF16)],
        compiler_params=_cparams("arbitrary", "arbitrary"),
        name="inproj",
    )(x2d, mod3, norm_g.reshape(1, D), w_all, w_ba, alog_lanes, dtb_lanes)


def _pool_kernel(u_ref, halo_ref, w_ref, sc_ref, o_ref, *, ts, gd):
    i = pl.program_id(1)
    halo = jnp.where(i > 0, halo_ref[...], 0.0)
    tpos = lax.broadcasted_iota(jnp.int32, (ts, 1), 0) + i * ts + 1
    for g in range(POOL_GROUPS):
        win = POOL_WINDOWS[g]
        u = u_ref[:, g * gd:(g + 1) * gd]
        s = jnp.concatenate([halo[:, g * gd:(g + 1) * gd], u], axis=0)
        k = 1
        while k < win:
            s = s + pltpu.roll(s, k, axis=0)
            k *= 2
        cnt = jnp.minimum(tpos, win).astype(F32)
        diff = s[POOL_HALO:, :] / cnt - u
        y = _dot(diff.astype(BF16), w_ref[g].astype(BF16)) * sc_ref[:, g * gd:(g + 1) * gd]
        o_ref[:, g * gd:(g + 1) * gd] = y.astype(o_ref.dtype)


def _pool(proj, pool_w, pool_scale, batch, seq):
    T = proj.shape[0]
    G, gd, _ = pool_w.shape
    width = G * gd
    ts = 512
    nt = seq // ts
    hb = ts // POOL_HALO
    return pl.pallas_call(
        functools.partial(_pool_kernel, ts=ts, gd=gd),
        grid=(batch, nt),
        in_specs=[pl.BlockSpec((ts, width), lambda b, i: (b * nt + i, 0)),
                  pl.BlockSpec((POOL_HALO, width),
                               lambda b, i: (jnp.maximum((b * nt + i) * hb - 1, 0), 0)),
                  pl.BlockSpec((G, gd, gd), lambda b, i: (0, 0, 0)),
                  pl.BlockSpec((1, width), lambda b, i: (0, 0))],
        out_specs=pl.BlockSpec((ts, width), lambda b, i: (b * nt + i, 0)),
        out_shape=jax.ShapeDtypeStruct((T, width), BF16),
        compiler_params=_cparams("arbitrary", "arbitrary"),
        name="pool",
    )(proj, proj, pool_w, pool_scale.reshape(1, width))


def _bmm(a, b):
    return jnp.einsum('nij,njk->nik', a.astype(BF16), b.astype(BF16),
                      preferred_element_type=F32)


def _bmm_nt(a, b):
    return jnp.einsum('nid,njd->nij', a.astype(BF16), b.astype(BF16),
                      preferred_element_type=F32)


def _dn_prepare(q_raw, k_raw, v_raw, cwq, cwk, cwv, beta_col, gc_col, *, chunk):
    seq, d = q_raw.shape
    n = seq // chunk
    top = lax.broadcasted_iota(jnp.int32, (8, d), 0)

    def shift(a, sh):
        r = pltpu.roll(a, sh, axis=0)
        return jnp.concatenate([jnp.where(top >= sh, r[:8], 0.0), r[8:]], axis=0)

    def conv_silu(x, cw):
        assert CONV_WIDTH == 4
        w0, w1, w2, w3 = (cw[j:j + 1, :] for j in range(CONV_WIDTH))
        x1 = shift(x, 1)
        return _silu(x * w3 + x1 * w2 + shift(x * w1 + x1 * w0, 2))

    def l2n(x):
        return x * lax.rsqrt(jnp.sum(x * x, axis=-1, keepdims=True) + EPS)

    qn = l2n(conv_silu(q_raw, cwq)) * (d ** -0.5)
    kn = l2n(conv_silu(k_raw, cwk))
    v = conv_silu(v_raw, cwv)
    beta = jnp.broadcast_to(beta_col, (seq, d))
    gc = jnp.broadcast_to(gc_col, (seq, d))

    gc3 = gc.reshape(n, chunk, d)
    ii = lax.broadcasted_iota(jnp.int32, (chunk, chunk), 0)
    jj = lax.broadcasted_iota(jnp.int32, (chunk, chunk), 1)
    incl = (ii >= jj)[None]
    strict = (ii > jj)[None]
    diff = gc3 - jnp.swapaxes(gc3, 1, 2)
    decay = jnp.where(incl, jnp.exp(jnp.where(incl, diff, 0.0)), 0.0)

    kb = kn * beta
    kn3 = kn.reshape(n, chunk, d)
    lmat = jnp.where(strict, _bmm_nt(kb.reshape(n, chunk, d), kn3) * decay, 0.0)
    qk = jnp.where(incl, _bmm_nt(qn.reshape(n, chunk, d), kn3) * decay, 0.0)

    def merge_mask(lv):
        same = (ii >> (lv + 1)) == (jj >> (lv + 1))
        return (same & (((ii >> lv) & 1) == 1) & (((jj >> lv) & 1) == 0))[None]

    tinv = (ii == jj).astype(F32)[None] - jnp.where(merge_mask(0), lmat, 0.0)
    for lv in range(1, chunk.bit_length() - 1):
        a21 = jnp.where(merge_mask(lv), lmat, 0.0)
        tinv = tinv - _bmm(tinv, _bmm(a21, tinv))

    egc = jnp.exp(gc)
    rhs = jnp.concatenate([v * beta, kb * egc], axis=-1).reshape(n, chunk, 2 * d)
    uw = _bmm(tinv, rhs)
    glast = gc3[:, chunk - 1:chunk, :]
    kdt = jnp.swapaxes(kn3 * jnp.exp(glast - gc3), 1, 2)
    kuw = _bmm(kdt, uw)
    quw = _bmm(qk, uw)
    qp = (qn * egc).reshape(n, chunk, d) - quw[..., d:]
    return quw[..., :d], qp, kuw[..., :d], kuw[..., d:], jnp.exp(glast)


def _dn_kernel(q_ref, k_ref, v_ref, z_ref, gates_ref, cwq_ref, cwk_ref, cwv_ref, on_ref,
               o_ref, o_scr, *, chunk, heads_per_step):
    d = DN_HEAD_DIM
    seq = q_ref.shape[0]
    n = seq // chunk
    lane = lax.broadcasted_iota(jnp.int32, (seq, LANES), 1)
    gates = gates_ref[...]
    prepared = []
    for i in range(heads_per_step):
        head = pl.program_id(1) * heads_per_step + i
        cols = slice(i * d, (i + 1) * d)
        beta_col = jnp.sum(jnp.where(lane == head, gates, 0.0), axis=-1, keepdims=True)
        gc_col = jnp.sum(jnp.where(lane == head + DN_HEADS, gates, 0.0), axis=-1, keepdims=True)
        prepared.append(_dn_prepare(q_ref[:, cols], k_ref[:, cols], v_ref[:, cols],
                                    cwq_ref[:, cols], cwk_ref[:, cols], cwv_ref[:, cols],
                                    beta_col, gc_col, chunk=chunk))

    states = [jnp.zeros((d, d), F32) for _ in range(heads_per_step)]
    for c in range(n):
        for i, (o0, qp, kub, kuw, eglast) in enumerate(prepared):
            sb = states[i].astype(BF16)
            o_scr[c * chunk:(c + 1) * chunk, i * d:(i + 1) * d] = o0[c] + _dot(qp[c].astype(BF16), sb)
            states[i] = states[i] * eglast[c] + kub[c] - _dot(kuw[c].astype(BF16), sb)

    for i in range(heads_per_step):
        cols = slice(i * d, (i + 1) * d)
        o = o_scr[:, cols]
        o = o * lax.rsqrt(jnp.mean(o * o, axis=-1, keepdims=True) + EPS) * on_ref[...]
        o_ref[:, cols] = (o * _silu(z_ref[:, cols])).astype(o_ref.dtype)


def _deltanet(proj, gates, conv_w, o_norm_g, batch, seq, col0):
    T = proj.shape[0]
    d = DN_HEAD_DIM
    H = DN_HEADS
    hps = 2
    w = hps * d
    cb = col0 // w
    nhb = H // hps

    def colspec(off):
        return pl.BlockSpec((seq, w), lambda b, h: (b, off + h))

    def convspec(off):
        return pl.BlockSpec((CONV_WIDTH, w), lambda b, h: (0, off + h))

    return pl.pallas_call(
        functools.partial(_dn_kernel, chunk=DN_CHUNK, heads_per_step=hps),
        grid=(batch, nhb),
        in_specs=[colspec(cb), colspec(cb + nhb), colspec(cb + 2 * nhb), colspec(cb + 3 * nhb),
                  pl.BlockSpec((seq, LANES), lambda b, h: (b, 0)),
                  convspec(0), convspec(nhb), convspec(2 * nhb),
                  pl.BlockSpec((1, d), lambda b, h: (0, 0))],
        out_specs=pl.BlockSpec((seq, w), lambda b, h: (b, h)),
        out_shape=jax.ShapeDtypeStruct((T, H * d), BF16),
        scratch_shapes=[pltpu.VMEM((seq, w), F32)],
        compiler_params=_cparams("arbitrary", "arbitrary"),
        name="deltanet",
    )(proj, proj, proj, proj, gates, conv_w, conv_w, conv_w, o_norm_g.reshape(1, d))


def _route(logits):
    lane = lax.broadcasted_iota(jnp.int32, logits.shape, 1)
    lanef = lane.astype(F32)
    far = float(LANES)

    def first_max(vals):
        m = jnp.max(vals, axis=-1, keepdims=True)
        idx = jnp.min(jnp.where(vals == m, lanef, far), axis=-1, keepdims=True)
        return m, idx

    gl = jnp.where(lane < N_GROUPS, logits, NEG_BIG)
    gmax, gidx = first_max(gl)
    p_top = 1.0 / jnp.sum(jnp.exp(gl - gmax), axis=-1, keepdims=True)
    lo = N_GROUPS + EXPERTS_PER_GROUP * gidx
    el = jnp.where(lanef >= lo, jnp.where(lanef < lo + EXPERTS_PER_GROUP, logits, NEG_BIG), NEG_BIG)
    m1, i1 = first_max(el)
    m2, i2 = first_max(jnp.where(lanef == i1, NEG_BIG, el))
    t = jnp.exp(m2 - m1)
    w1 = p_top / (1.0 + t)
    w2 = w1 * t
    return jnp.where(lane == 0, i1 - N_GROUPS,
                     jnp.where(lane == 1, i2 - N_GROUPS,
                               jnp.where(lane == 2, w1, jnp.where(lane == 3, w2, 0.0))))


def _slab_store(ref, val, ns=None, first_slab=0):
    rows, width = val.shape
    ns = ns or width // LANES
    for s in range(width // LANES):
        ref[pl.ds(first_slab + s, rows, stride=ns), :] = val[:, s * LANES:(s + 1) * LANES]


def _slab_load(ref, first_row, rows, ns):
    return [ref[pl.ds(first_row * ns + s, rows, stride=ns), :] for s in range(ns)]


def _outproj_kernel(yp_ref, yd_ref, wo_ref, x_ref, mod_ref, g_ref, wr_ref, br_ref,
                    x2_ref, h2_ref, route_ref):
    half = yp_ref.shape[1]
    out = _dot(yp_ref[...], wo_ref[:half, :]) + _dot(yd_ref[...], wo_ref[half:, :])
    x2 = x_ref[...] + mod_ref[0, 2:3, :] * out
    x2_ref[...] = x2
    ms = jnp.mean(x2 * x2, axis=-1, keepdims=True)
    y = x2 * lax.rsqrt(ms + EPS) * g_ref[...]
    h2 = y * (1.0 + mod_ref[0, 4:5, :]) + mod_ref[0, 3:4, :]
    _slab_store(h2_ref, h2)
    hi = h2.astype(BF16)
    lo = (h2 - hi.astype(F32)).astype(BF16)
    both = _dot(hi, wr_ref[...])
    logits = both[:, :LANES] + (_dot(lo, wr_ref[:, :LANES]) + both[:, LANES:]) + br_ref[...]
    route_ref[...] = _route(logits)


def _outproj(y_pool, y_dn, w_out_bf, x2d, mod3, norm_g, w_router, b_router, seq):
    T, D = x2d.shape
    half = y_pool.shape[1]
    tm = 256
    row = lambda i: (i, 0)
    const = lambda i: (0, 0)
    return pl.pallas_call(
        _outproj_kernel,
        grid=(T // tm,),
        in_specs=[pl.BlockSpec((tm, half), row),
                  pl.BlockSpec((tm, half), row),
                  pl.BlockSpec((2 * half, D), const),
                  pl.BlockSpec((tm, D), row),
                  pl.BlockSpec((1, 6, D), lambda i: (i * tm // seq, 0, 0)),
                  pl.BlockSpec((1, D), const),
                  pl.BlockSpec((D, 2 * LANES), const),
                  pl.BlockSpec((1, LANES), const)],
        out_specs=[pl.BlockSpec((tm, D), row),
                   pl.BlockSpec((tm * (D // LANES), LANES), row),
                   pl.BlockSpec((tm, LANES), row)],
        out_shape=[jax.ShapeDtypeStruct((T, D), F32),
                   jax.ShapeDtypeStruct((T * (D // LANES), LANES), F32),
                   jax.ShapeDtypeStruct((T, LANES), F32)],
        compiler_params=_cparams("arbitrary"),
        name="outproj",
    )(y_pool, y_dn, w_out_bf, x2d, mod3, norm_g.reshape(1, D), w_router, b_router)


GATHER_UNROLL = 8


def _row_gather(idx_ref, nrows, ns, src_hbm, dst, sem):
    def body(r, carry):
        src_row = pl.multiple_of(idx_ref[0, 0, r] * ns, ns)
        dst_row = pl.multiple_of(r * ns, ns)
        pltpu.make_async_copy(src_hbm.at[pl.ds(src_row, ns), :], dst.at[pl.ds(dst_row, ns), :],
                              sem).start()
        return carry
    lax.fori_loop(0, nrows, body, 0, unroll=GATHER_UNROLL)


class _RowIssuer:
    def __init__(self, idx_ref, nrows, ns, src_hbm, dst, sem):
        self.args = (idx_ref, ns, src_hbm, dst, sem)
        self.nrows = nrows
        self.done = 0

    def __call__(self, count):
        idx_ref, ns, src_hbm, dst, sem = self.args
        stop = min(self.done + count, self.nrows)
        for r in range(self.done, stop):
            src_row = pl.multiple_of(idx_ref[0, 0, r] * ns, ns)
            pltpu.make_async_copy(src_hbm.at[pl.ds(src_row, ns), :],
                                  dst.at[pl.ds(r * ns, ns), :], sem).start()
        self.done = stop


def _row_gather_wait(nrows, ns, src_hbm, dst, sem):
    pltpu.make_async_copy(src_hbm.at[pl.ds(0, nrows * ns), :], dst, sem).wait()


WEIGHT_DMA_PRIORITY = 1


def _expert_kernel(be_ref, first_ref, wslot_ref, nexte_ref, nact_ref, tok_ref, tokn_ref,
                   h_hbm, wg_hbm, wu_hbm, wd_hbm, ys_ref, xbuf, wg_buf, wu_buf, wd_buf,
                   sem, wsem, *, bm, ns):
    i = pl.program_id(0)
    nact = nact_ref[0]
    slot = i % 2
    wslot = wslot_ref[i]

    def weight_copies(e, s):
        return (pltpu.make_async_copy(wg_hbm.at[e], wg_buf.at[s], wsem.at[s]),
                pltpu.make_async_copy(wu_hbm.at[e], wu_buf.at[s], wsem.at[s]),
                pltpu.make_async_copy(wd_hbm.at[e], wd_buf.at[s], wsem.at[s]))

    @pl.when(i == 0)
    def _():
        for cp in weight_copies(be_ref[0], 0):
            cp.start(priority=WEIGHT_DMA_PRIORITY)
        _row_gather(tok_ref, bm, ns, h_hbm, xbuf.at[0], sem.at[0])

    @pl.when((first_ref[i] == 1) & (nexte_ref[i] >= 0))
    def _():
        for cp in weight_copies(nexte_ref[i], 1 - wslot):
            cp.start(priority=WEIGHT_DMA_PRIORITY)

    @pl.when(first_ref[i] == 1)
    def _():
        for cp in weight_copies(be_ref[i], wslot):
            cp.wait()

    @pl.when(i <= nact)
    def _():
        _row_gather_wait(bm, ns, h_hbm, xbuf.at[slot], sem.at[slot])

    @pl.when(i < nact)
    def _():
        issue = _RowIssuer(tokn_ref, bm, ns, h_hbm, xbuf.at[1 - slot], sem.at[1 - slot])
        de, d = wd_buf.shape[1], wd_buf.shape[2]
        pieces = 2 * (de // MXU_COLS) + d // MXU_COLS
        per = -(-bm // pieces)
        xb = jnp.concatenate(_slab_load(xbuf.at[slot], 0, bm, ns), axis=-1).astype(BF16)
        hid = []
        for c in range(de // MXU_COLS):
            cols = slice(c * MXU_COLS, (c + 1) * MXU_COLS)
            gate = _dot(xb, wg_buf[wslot, :, cols].astype(BF16))
            issue(per)
            up = _dot(xb, wu_buf[wslot, :, cols].astype(BF16))
            issue(per)
            hid.append((_silu(gate) * up).astype(BF16))
        hid = jnp.concatenate(hid, axis=-1)
        for c in range(d // MXU_COLS):
            cols = slice(c * MXU_COLS, (c + 1) * MXU_COLS)
            y = _dot(hid, wd_buf[wslot, :, cols].astype(BF16))
            _slab_store(ys_ref, y, ns, c * (MXU_COLS // LANES))
            issue(per)
        issue(bm)

    @pl.when(i >= nact)
    def _():
        ys_ref[...] = jnp.zeros_like(ys_ref)


def _experts(h2_slab, block_e, nact, buf_tok, w_gate, w_up, w_down):
    E, D, De = w_gate.shape
    ns = D // LANES
    bm = MOE_BLOCK
    n_pad = buf_tok.shape[0]
    nb = n_pad // bm
    tok3 = buf_tok.reshape(nb, 1, bm)

    idx = jnp.arange(nb, dtype=jnp.int32)
    active = idx < nact[0]
    prev_e = jnp.concatenate([jnp.full((1,), -1, jnp.int32), block_e[:-1]])
    first = (active & (block_e != prev_e)).astype(jnp.int32)
    wslot = ((jnp.cumsum(first) - 1) % 2).astype(jnp.int32)
    later_first = (first[None, :] == 1) & (idx[None, :] > idx[:, None])
    nxt = jnp.min(jnp.where(later_first, idx[None, :], nb), axis=1)
    next_e = jnp.where(nxt < nb, block_e[jnp.minimum(nxt, nb - 1)], -1).astype(jnp.int32)

    any_spec = pl.BlockSpec(memory_space=pl.ANY)
    grid_spec = pltpu.PrefetchScalarGridSpec(
        num_scalar_prefetch=5,
        grid=(nb,),
        in_specs=[pl.BlockSpec((1, 1, bm), lambda i, *_: (i, 0, 0), memory_space=pltpu.SMEM),
                  pl.BlockSpec((1, 1, bm), lambda i, *_: (jnp.minimum(i + 1, nb - 1), 0, 0),
                               memory_space=pltpu.SMEM),
                  any_spec, any_spec, any_spec, any_spec],
        out_specs=pl.BlockSpec((bm * ns, LANES), lambda i, *_: (i, 0)),
        scratch_shapes=[pltpu.VMEM((2, bm * ns, LANES), F32),
                        pltpu.VMEM((2, D, De), F32), pltpu.VMEM((2, D, De), F32),
                        pltpu.VMEM((2, De, D), F32),
                        pltpu.SemaphoreType.DMA((2,)), pltpu.SemaphoreType.DMA((2,))],
    )
    return pl.pallas_call(
        functools.partial(_expert_kernel, bm=bm, ns=ns),
        grid_spec=grid_spec,
        out_shape=jax.ShapeDtypeStruct((n_pad * ns, LANES), F32),
        compiler_params=_cparams("arbitrary"),
        name="experts",
    )(block_e, first, wslot, next_e, nact, tok3, tok3, h2_slab, w_gate, w_up, w_down)


def _combine_kernel(pos_ref, posn_ref, ys_hbm, x2_ref, mod_ref, route_ref, g_ref, o_ref,
                    ybuf, sem, *, tm, ns):
    i = pl.program_id(0)
    nsteps = pl.num_programs(0)
    slot = i % 2
    nrows = TOP_K * tm

    @pl.when(i == 0)
    def _():
        _row_gather(pos_ref, nrows, ns, ys_hbm, ybuf.at[0], sem.at[0])

    @pl.when(i + 1 < nsteps)
    def _():
        _row_gather(posn_ref, nrows, ns, ys_hbm, ybuf.at[1 - slot], sem.at[1 - slot])

    _row_gather_wait(nrows, ns, ys_hbm, ybuf.at[slot], sem.at[slot])
    route = route_ref[...]
    w0 = route[:, 2:3]
    w1 = route[:, 3:4]
    y0 = _slab_load(ybuf.at[slot], 0, tm, ns)
    y1 = _slab_load(ybuf.at[slot], tm, tm, ns)
    y = jnp.concatenate([w0 * a + w1 * b for a, b in zip(y0, y1)], axis=-1)
    x3 = x2_ref[...] + mod_ref[0, 5:6, :] * y
    ms = jnp.mean(x3 * x3, axis=-1, keepdims=True)
    o_ref[...] = x3 * lax.rsqrt(ms + EPS) * g_ref[...]


def _combine(ys_slab, pos, x2, mod3, route, norm_g, seq):
    T, D = x2.shape
    ns = D // LANES
    tm = 256
    nt = T // tm
    pos3 = pos.reshape(nt, tm, TOP_K).transpose(0, 2, 1).reshape(nt, 1, TOP_K * tm)
    row = lambda i: (i, 0)
    return pl.pallas_call(
        functools.partial(_combine_kernel, tm=tm, ns=ns),
        grid=(nt,),
        in_specs=[pl.BlockSpec((1, 1, TOP_K * tm), lambda i: (i, 0, 0), memory_space=pltpu.SMEM),
                  pl.BlockSpec((1, 1, TOP_K * tm), lambda i: (jnp.minimum(i + 1, nt - 1), 0, 0),
                               memory_space=pltpu.SMEM),
                  pl.BlockSpec(memory_space=pl.ANY),
                  pl.BlockSpec((tm, D), row),
                  pl.BlockSpec((1, 6, D), lambda i: (i * tm // seq, 0, 0)),
                  pl.BlockSpec((tm, LANES), row),
                  pl.BlockSpec((1, D), lambda i: (0, 0))],
        out_specs=pl.BlockSpec((tm, D), row),
        out_shape=jax.ShapeDtypeStruct((T, D), F32),
        scratch_shapes=[pltpu.VMEM((2, TOP_K * tm * ns, LANES), F32),
                        pltpu.SemaphoreType.DMA((2,))],
        compiler_params=_cparams("arbitrary"),
        name="combine",
    )(pos3, pos3, ys_slab, x2, mod3, route, norm_g.reshape(1, D))


def _dispatch_plan(route, n_tokens):
    A = n_tokens * TOP_K
    bm = MOE_BLOCK
    nb = (A + N_EXPERTS * (bm - 1)) // bm + 1
    flat_e = route[:, :TOP_K].astype(jnp.int32).reshape(A)
    iota = jnp.arange(A, dtype=jnp.int32)
    _, order = lax.sort_key_val(flat_e, iota)
    _, inv = lax.sort_key_val(order, iota)
    onehot = flat_e[:, None] == jnp.arange(N_EXPERTS, dtype=jnp.int32)[None, :]
    counts = lax.optimization_barrier(jnp.sum(onehot, axis=0, dtype=jnp.int32))
    padded = (counts + bm - 1) // bm * bm
    pad_end = jnp.cumsum(padded)
    pad_start = pad_end - padded
    start = jnp.cumsum(counts) - counts
    pos = inv + jnp.sum(jnp.where(onehot, (pad_start - start)[None, :], 0), axis=1)
    block_start = jnp.arange(nb, dtype=jnp.int32) * bm
    block_e = jnp.minimum(jnp.sum(pad_end[None, :] <= block_start[:, None], axis=1),
                          N_EXPERTS - 1).astype(jnp.int32)
    r = (block_start - pad_start[block_e])[:, None] + jnp.arange(bm, dtype=jnp.int32)[None, :]
    src = jnp.clip(start[block_e][:, None] + r, 0, A - 1)
    buf_tok = jnp.where(r < counts[block_e][:, None], order[src] // TOP_K, 0).reshape(nb * bm)
    nact = (pad_end[-1:] // bm).astype(jnp.int32)
    block_e, nact = lax.optimization_barrier((block_e, nact))
    return buf_tok, pos.astype(jnp.int32), block_e, nact


def kernel(x, c, w_ada, b_ada, norm1_g, w_in, pool_w, pool_scale, conv_w, a_log, dt_bias,
           o_norm_g, w_out, norm2_g, w_router_group, b_router_group, w_router_expert,
           b_router_expert, w_gate, w_up, w_down, norm_f_g):
    B, S, D = x.shape
    T = B * S
    depth = w_ada.shape[0]
    pool_width = pool_w.shape[1] * pool_w.shape[2]
    n_main = pool_width + 4 * DN_HEADS * DN_HEAD_DIM
    n_route = N_GROUPS + N_EXPERTS

    assert depth == 1, "kernel supports the single-layer configuration only"
    l = 0
    xt = x.reshape(T, D)
    mod3 = _ada(c, w_ada[l], b_ada[l]).reshape(B, 6, D)

    w_all = w_in[l].astype(BF16)
    w_ba = jnp.pad(w_in[l, :, n_main:], ((0, 0), (0, LANES - 2 * DN_HEADS))).astype(BF16)
    gate_pad = (DN_HEADS, LANES - 2 * DN_HEADS)
    alog_lanes = jnp.pad(a_log[l], gate_pad).reshape(1, LANES)
    dtb_lanes = jnp.pad(dt_bias[l], gate_pad).reshape(1, LANES)
    proj, gates = _inproj(xt, mod3, norm1_g[l], w_all, n_main, w_ba, alog_lanes, dtb_lanes, S)

    y_pool = _pool(proj, pool_w[l], pool_scale[l], B, S)
    y_dn = _deltanet(proj, gates, conv_w[l], o_norm_g[l], B, S, pool_width)

    w_router = jnp.pad(jnp.concatenate([w_router_group[l], w_router_expert[l]], axis=1),
                       ((0, 0), (0, LANES - n_route)))
    b_router = jnp.pad(jnp.concatenate([b_router_group[l], b_router_expert[l]]),
                       (0, LANES - n_route)).reshape(1, LANES)
    w_router_hi = w_router.astype(BF16)
    w_router_lo = (w_router - w_router_hi.astype(F32)).astype(BF16)
    x2, h2, route = _outproj(y_pool, y_dn, w_out[l].astype(BF16), xt, mod3, norm2_g[l],
                             jnp.concatenate([w_router_hi, w_router_lo], axis=1), b_router, S)

    buf_tok, pos, block_e, nact = _dispatch_plan(route, T)
    ys = _experts(h2, block_e, nact, buf_tok, w_gate[l], w_up[l], w_down[l])
    out = _combine(ys, pos, x2, mod3, route, norm_f_g, S)
    return out.reshape(B, S, D)
```

```python
import functools

import jax
import jax.numpy as jnp
from jax import lax
from jax.experimental import pallas as pl
from jax.experimental.pallas import tpu as pltpu

F32 = jnp.float32
BF16 = jnp.bfloat16

POOL_GROUPS = 4
POOL_WINDOWS = (2, 4, 8, 16)
POOL_HALO = 16
DN_HEADS = 8
DN_HEAD_DIM = 128
CONV_WIDTH = 4
DN_CHUNK = 128
N_GROUPS = 4
EXPERTS_PER_GROUP = 8
N_EXPERTS = N_GROUPS * EXPERTS_PER_GROUP
TOP_K = 2
MOE_BLOCK = 256
EPS = 1e-6
LANES = 128
NEG_BIG = -3.0e38
VMEM_LIMIT = 60 * 1024 * 1024


def _silu(x):
    half = 0.5 * x
    return half * (1.0 + jnp.tanh(half))


def _chunk_cumsum(x, chunk):
    rmod = lax.broadcasted_iota(jnp.int32, x.shape, 0) & (chunk - 1)
    k = 1
    while k < chunk:
        x = x + jnp.where(rmod >= k, pltpu.roll(x, k, axis=0), 0.0)
        k *= 2
    return x


def _dot(a, b):
    return jnp.dot(a, b, preferred_element_type=F32)


def _cparams(*sem):
    return pltpu.CompilerParams(dimension_semantics=sem, vmem_limit_bytes=VMEM_LIMIT)


def _ada_kernel(c_ref, w_ref, b_ref, o_ref):
    ca = _silu(c_ref[...])
    nb = ca.shape[0]
    c_hi = ca.astype(BF16).astype(F32)
    hi_lo = jnp.concatenate([c_hi, ca - c_hi], axis=0).astype(BF16)
    w = w_ref[...]
    w_hi = w.astype(BF16)
    w_lo = (w - w_hi.astype(F32)).astype(BF16)
    both = _dot(hi_lo, w_hi)
    o_ref[...] = both[:nb] + (both[nb:] + _dot(c_hi.astype(BF16), w_lo)) + b_ref[...]


def _ada(c, w_ada, b_ada):
    B, D = c.shape
    N = w_ada.shape[1]
    tn = 1024
    return pl.pallas_call(
        _ada_kernel,
        grid=(N // tn,),
        in_specs=[pl.BlockSpec((B, D), lambda j: (0, 0)),
                  pl.BlockSpec((D, tn), lambda j: (0, j)),
                  pl.BlockSpec((1, tn), lambda j: (0, j))],
        out_specs=pl.BlockSpec((B, tn), lambda j: (0, j)),
        out_shape=jax.ShapeDtypeStruct((B, N), F32),
        compiler_params=_cparams("arbitrary"),
        name="ada",
    )(c, w_ada, b_ada.reshape(1, N))


def _inproj_kernel(x_ref, mod_ref, g_ref, w_ref, wba_ref, alog_ref, dtb_ref,
                   proj_ref, gates_ref, h_ref):
    @pl.when(pl.program_id(1) == 0)
    def _():
        x = x_ref[...]
        ms = jnp.mean(x * x, axis=-1, keepdims=True)
        y = x * lax.rsqrt(ms + EPS) * g_ref[...]
        h = (y * (1.0 + mod_ref[0, 1:2, :]) + mod_ref[0, 0:1, :]).astype(BF16)
        h_ref[...] = h
        proj_ref[...] = _dot(h, w_ref[...])
        ba = _dot(h, wba_ref[...])
        lane = lax.broadcasted_iota(jnp.int32, ba.shape, 1)
        a = ba + dtb_ref[...]
        softplus = jnp.maximum(a, 0.0) + jnp.log1p(jnp.exp(-jnp.abs(a)))
        gc = _chunk_cumsum(-jnp.exp(alog_ref[...]) * softplus, DN_CHUNK)
        gates_ref[...] = jnp.where(lane < DN_HEADS, jax.nn.sigmoid(ba), gc)

    @pl.when(pl.program_id(1) != 0)
    def _():
        proj_ref[...] = _dot(h_ref[...], w_ref[...])


def _inproj(x2d, mod3, norm_g, w_all, n_main, w_ba, alog_lanes, dtb_lanes, seq):
    T, D = x2d.shape
    N = n_main
    tm, tn = 1024, 1024
    assert tm % DN_CHUNK == 0 and seq % tm == 0 and N % tn == 0
    return pl.pallas_call(
        _inproj_kernel,
        grid=(T // tm, N // tn),
        in_specs=[pl.BlockSpec((tm, D), lambda i, j: (i, 0)),
                  pl.BlockSpec((1, 6, D), lambda i, j: (i * tm // seq, 0, 0)),
                  pl.BlockSpec((1, D), lambda i, j: (0, 0)),
                  pl.BlockSpec((D, tn), lambda i, j: (0, j)),
                  pl.BlockSpec((D, LANES), lambda i, j: (0, 0)),
                  pl.BlockSpec((1, LANES), lambda i, j: (0, 0)),
                  pl.BlockSpec((1, LANES), lambda i, j: (0, 0))],
        out_specs=[pl.BlockSpec((tm, tn), lambda i, j: (i, j)),
                   pl.BlockSpec((tm, LANES), lambda i, j: (i, 0))],
        out_shape=[jax.ShapeDtypeStruct((T, N), F32),
                   jax.ShapeDtypeStruct((T, LANES), F32)],
        scratch_shapes=[pltpu.VMEM((tm, D), BF16)],
        compiler_params=_cparams("arbitrary", "arbitrary"),
        name="inproj",
    )(x2d, mod3, norm_g.reshape(1, D), w_all, w_ba, alog_lanes, dtb_lanes)


def _pool_kernel(u_ref, halo_ref, w_ref, sc_ref, o_ref, *, ts, gd):
    i = pl.program_id(1)
    halo = jnp.where(i > 0, halo_ref[...], 0.0)
    tpos = lax.broadcasted_iota(jnp.int32, (ts, 1), 0) + i * ts + 1
    for g in range(POOL_GROUPS):
        win = POOL_WINDOWS[g]
        u = u_ref[:, g * gd:(g + 1) * gd]
        s = jnp.concatenate([halo[:, g * gd:(g + 1) * gd], u], axis=0)
        k = 1
        while k < win:
            s = s + pltpu.roll(s, k, axis=0)
            k *= 2
        cnt = jnp.minimum(tpos, win).astype(F32)
        diff = s[POOL_HALO:, :] / cnt - u
        y = _dot(diff.astype(BF16), w_ref[g].astype(BF16)) * sc_ref[:, g * gd:(g + 1) * gd]
        o_ref[:, g * gd:(g + 1) * gd] = y.astype(o_ref.dtype)


def _pool(proj, pool_w, pool_scale, batch, seq):
    T = proj.shape[0]
    G, gd, _ = pool_w.shape
    width = G * gd
    ts = 512
    nt = seq // ts
    hb = ts // POOL_HALO
    return pl.pallas_call(
        functools.partial(_pool_kernel, ts=ts, gd=gd),
        grid=(batch, nt),
        in_specs=[pl.BlockSpec((ts, width), lambda b, i: (b * nt + i, 0)),
                  pl.BlockSpec((POOL_HALO, width),
                               lambda b, i: (jnp.maximum((b * nt + i) * hb - 1, 0), 0)),
                  pl.BlockSpec((G, gd, gd), lambda b, i: (0, 0, 0)),
                  pl.BlockSpec((1, width), lambda b, i: (0, 0))],
        out_specs=pl.BlockSpec((ts, width), lambda b, i: (b * nt + i, 0)),
        out_shape=jax.ShapeDtypeStruct((T, width), BF16),
        compiler_params=_cparams("arbitrary", "arbitrary"),
        name="pool",
    )(proj, proj, pool_w, pool_scale.reshape(1, width))


def _bmm(a, b):
    return jnp.einsum('nij,njk->nik', a.astype(BF16), b.astype(BF16),
                      preferred_element_type=F32)


def _bmm_nt(a, b):
    return jnp.einsum('nid,njd->nij', a.astype(BF16), b.astype(BF16),
                      preferred_element_type=F32)


def _dn_prepare(q_raw, k_raw, v_raw, cwq, cwk, cwv, beta_col, gc_col, *, chunk):
    seq, d = q_raw.shape
    n = seq // chunk
    top = lax.broadcasted_iota(jnp.int32, (8, d), 0)

    def shift(a, sh):
        r = pltpu.roll(a, sh, axis=0)
        return jnp.concatenate([jnp.where(top >= sh, r[:8], 0.0), r[8:]], axis=0)

    def conv_silu(x, cw):
        assert CONV_WIDTH == 4
        w0, w1, w2, w3 = (cw[j:j + 1, :] for j in range(CONV_WIDTH))
        x1 = shift(x, 1)
        return _silu(x * w3 + x1 * w2 + shift(x * w1 + x1 * w0, 2))

    def l2n(x):
        return x * lax.rsqrt(jnp.sum(x * x, axis=-1, keepdims=True) + EPS)

    qn = l2n(conv_silu(q_raw, cwq)) * (d ** -0.5)
    kn = l2n(conv_silu(k_raw, cwk))
    v = conv_silu(v_raw, cwv)
    beta = jnp.broadcast_to(beta_col, (seq, d))
    gc = jnp.broadcast_to(gc_col, (seq, d))

    gc3 = gc.reshape(n, chunk, d)
    ii = lax.broadcasted_iota(jnp.int32, (chunk, chunk), 0)
    jj = lax.broadcasted_iota(jnp.int32, (chunk, chunk), 1)
    incl = (ii >= jj)[None]
    strict = (ii > jj)[None]
    diff = gc3 - jnp.swapaxes(gc3, 1, 2)
    decay = jnp.where(incl, jnp.exp(jnp.where(incl, diff, 0.0)), 0.0)

    kb = kn * beta
    kn3 = kn.reshape(n, chunk, d)
    lmat = jnp.where(strict, _bmm_nt(kb.reshape(n, chunk, d), kn3) * decay, 0.0)
    qk = jnp.where(incl, _bmm_nt(qn.reshape(n, chunk, d), kn3) * decay, 0.0)

    def merge_mask(lv):
        same = (ii >> (lv + 1)) == (jj >> (lv + 1))
        return (same & (((ii >> lv) & 1) == 1) & (((jj >> lv) & 1) == 0))[None]

    tinv = (ii == jj).astype(F32)[None] - jnp.where(merge_mask(0), lmat, 0.0)
    for lv in range(1, chunk.bit_length() - 1):
        a21 = jnp.where(merge_mask(lv), lmat, 0.0)
        tinv = tinv - _bmm(tinv, _bmm(a21, tinv))

    egc = jnp.exp(gc)
    rhs = jnp.concatenate([v * beta, kb * egc], axis=-1).reshape(n, chunk, 2 * d)
    uw = _bmm(tinv, rhs)
    glast = gc3[:, chunk - 1:chunk, :]
    kdt = jnp.swapaxes(kn3 * jnp.exp(glast - gc3), 1, 2)
    kuw = _bmm(kdt, uw)
    quw = _bmm(qk, uw)
    qp = (qn * egc).reshape(n, chunk, d) - quw[..., d:]
    return quw[..., :d], qp, kuw[..., :d], kuw[..., d:], jnp.exp(glast)


def _dn_kernel(q_ref, k_ref, v_ref, z_ref, gates_ref, cwq_ref, cwk_ref, cwv_ref, on_ref,
               o_ref, o_scr, *, chunk, heads_per_step):
    d = DN_HEAD_DIM
    seq = q_ref.shape[0]
    n = seq // chunk
    lane = lax.broadcasted_iota(jnp.int32, (seq, LANES), 1)
    gates = gates_ref[...]
    prepared = []
    for i in range(heads_per_step):
        head = pl.program_id(1) * heads_per_step + i
        cols = slice(i * d, (i + 1) * d)
        beta_col = jnp.sum(jnp.where(lane == head, gates, 0.0), axis=-1, keepdims=True)
        gc_col = jnp.sum(jnp.where(lane == head + DN_HEADS, gates, 0.0), axis=-1, keepdims=True)
        prepared.append(_dn_prepare(q_ref[:, cols], k_ref[:, cols], v_ref[:, cols],
                                    cwq_ref[:, cols], cwk_ref[:, cols], cwv_ref[:, cols],
                                    beta_col, gc_col, chunk=chunk))

    states = [jnp.zeros((d, d), F32) for _ in range(heads_per_step)]
    for c in range(n):
        for i, (o0, qp, kub, kuw, eglast) in enumerate(prepared):
            sb = states[i].astype(BF16)
            o_scr[c * chunk:(c + 1) * chunk, i * d:(i + 1) * d] = o0[c] + _dot(qp[c].astype(BF16), sb)
            states[i] = states[i] * eglast[c] + kub[c] - _dot(kuw[c].astype(BF16), sb)

    for i in range(heads_per_step):
        cols = slice(i * d, (i + 1) * d)
        o = o_scr[:, cols]
        o = o * lax.rsqrt(jnp.mean(o * o, axis=-1, keepdims=True) + EPS) * on_ref[...]
        o_ref[:, cols] = (o * _silu(z_ref[:, cols])).astype(o_ref.dtype)


def _deltanet(proj, gates, conv_w, o_norm_g, batch, seq, col0):
    T = proj.shape[0]
    d = DN_HEAD_DIM
    H = DN_HEADS
    hps = 2
    w = hps * d
    cb = col0 // w
    nhb = H // hps

    def colspec(off):
        return pl.BlockSpec((seq, w), lambda b, h: (b, off + h))

    def convspec(off):
        return pl.BlockSpec((CONV_WIDTH, w), lambda b, h: (0, off + h))

    return pl.pallas_call(
        functools.partial(_dn_kernel, chunk=DN_CHUNK, heads_per_step=hps),
        grid=(batch, nhb),
        in_specs=[colspec(cb), colspec(cb + nhb), colspec(cb + 2 * nhb), colspec(cb + 3 * nhb),
                  pl.BlockSpec((seq, LANES), lambda b, h: (b, 0)),
                  convspec(0), convspec(nhb), convspec(2 * nhb),
                  pl.BlockSpec((1, d), lambda b, h: (0, 0))],
        out_specs=pl.BlockSpec((seq, w), lambda b, h: (b, h)),
        out_shape=jax.ShapeDtypeStruct((T, H * d), BF16),
        scratch_shapes=[pltpu.VMEM((seq, w), F32)],
        compiler_params=_cparams("arbitrary", "arbitrary"),
        name="deltanet",
    )(proj, proj, proj, proj, gates, conv_w, conv_w, conv_w, o_norm_g.reshape(1, d))


def _route(logits):
    lane = lax.broadcasted_iota(jnp.int32, logits.shape, 1)
    lanef = lane.astype(F32)
    far = float(LANES)

    def first_max(vals):
        m = jnp.max(vals, axis=-1, keepdims=True)
        idx = jnp.min(jnp.where(vals == m, lanef, far), axis=-1, keepdims=True)
        return m, idx

    gl = jnp.where(lane < N_GROUPS, logits, NEG_BIG)
    gmax, gidx = first_max(gl)
    p_top = 1.0 / jnp.sum(jnp.exp(gl - gmax), axis=-1, keepdims=True)
    lo = N_GROUPS + EXPERTS_PER_GROUP * gidx
    el = jnp.where(lanef >= lo, jnp.where(lanef < lo + EXPERTS_PER_GROUP, logits, NEG_BIG), NEG_BIG)
    m1, i1 = first_max(el)
    m2, i2 = first_max(jnp.where(lanef == i1, NEG_BIG, el))
    t = jnp.exp(m2 - m1)
    w1 = p_top / (1.0 + t)
    w2 = w1 * t
    return jnp.where(lane == 0, i1 - N_GROUPS,
                     jnp.where(lane == 1, i2 - N_GROUPS,
                               jnp.where(lane == 2, w1, jnp.where(lane == 3, w2, 0.0))))


def _slab_store(ref, val):
    rows, width = val.shape
    ns = width // LANES
    for s in range(ns):
        ref[pl.ds(s, rows, stride=ns), :] = val[:, s * LANES:(s + 1) * LANES]


def _slab_load(ref, first_row, rows, ns):
    return [ref[pl.ds(first_row * ns + s, rows, stride=ns), :] for s in range(ns)]


def _outproj_kernel(yp_ref, yd_ref, wo_ref, x_ref, mod_ref, g_ref, wr_ref, br_ref,
                    x2_ref, h2_ref, route_ref):
    half = yp_ref.shape[1]
    out = _dot(yp_ref[...], wo_ref[:half, :]) + _dot(yd_ref[...], wo_ref[half:, :])
    x2 = x_ref[...] + mod_ref[0, 2:3, :] * out
    x2_ref[...] = x2
    ms = jnp.mean(x2 * x2, axis=-1, keepdims=True)
    y = x2 * lax.rsqrt(ms + EPS) * g_ref[...]
    h2 = y * (1.0 + mod_ref[0, 4:5, :]) + mod_ref[0, 3:4, :]
    _slab_store(h2_ref, h2)
    hi = h2.astype(BF16)
    lo = (h2 - hi.astype(F32)).astype(BF16)
    both = _dot(hi, wr_ref[...])
    logits = both[:, :LANES] + (_dot(lo, wr_ref[:, :LANES]) + both[:, LANES:]) + br_ref[...]
    route_ref[...] = _route(logits)


def _outproj(y_pool, y_dn, w_out_bf, x2d, mod3, norm_g, w_router, b_router, seq):
    T, D = x2d.shape
    half = y_pool.shape[1]
    tm = 256
    row = lambda i: (i, 0)
    const = lambda i: (0, 0)
    return pl.pallas_call(
        _outproj_kernel,
        grid=(T // tm,),
        in_specs=[pl.BlockSpec((tm, half), row),
                  pl.BlockSpec((tm, half), row),
                  pl.BlockSpec((2 * half, D), const),
                  pl.BlockSpec((tm, D), row),
                  pl.BlockSpec((1, 6, D), lambda i: (i * tm // seq, 0, 0)),
                  pl.BlockSpec((1, D), const),
                  pl.BlockSpec((D, 2 * LANES), const),
                  pl.BlockSpec((1, LANES), const)],
        out_specs=[pl.BlockSpec((tm, D), row),
                   pl.BlockSpec((tm * (D // LANES), LANES), row),
                   pl.BlockSpec((tm, LANES), row)],
        out_shape=[jax.ShapeDtypeStruct((T, D), F32),
                   jax.ShapeDtypeStruct((T * (D // LANES), LANES), F32),
                   jax.ShapeDtypeStruct((T, LANES), F32)],
        compiler_params=_cparams("arbitrary"),
        name="outproj",
    )(y_pool, y_dn, w_out_bf, x2d, mod3, norm_g.reshape(1, D), w_router, b_router)


GATHER_UNROLL = 8


def _row_gather(idx_ref, nrows, ns, src_hbm, dst, sem):
    def body(r, carry):
        src_row = pl.multiple_of(idx_ref[0, 0, r] * ns, ns)
        dst_row = pl.multiple_of(r * ns, ns)
        pltpu.make_async_copy(src_hbm.at[pl.ds(src_row, ns), :], dst.at[pl.ds(dst_row, ns), :],
                              sem).start()
        return carry
    lax.fori_loop(0, nrows, body, 0, unroll=GATHER_UNROLL)


def _row_gather_wait(nrows, ns, src_hbm, dst, sem):
    pltpu.make_async_copy(src_hbm.at[pl.ds(0, nrows * ns), :], dst, sem).wait()


WEIGHT_DMA_PRIORITY = 1


def _expert_kernel(be_ref, first_ref, wslot_ref, nexte_ref, nact_ref, tok_ref, tokn_ref,
                   h_hbm, wg_hbm, wu_hbm, wd_hbm, ys_ref, xbuf, wg_buf, wu_buf, wd_buf,
                   sem, wsem, *, bm, ns):
    i = pl.program_id(0)
    nact = nact_ref[0]
    slot = i % 2
    wslot = wslot_ref[i]

    def weight_copies(e, s):
        return (pltpu.make_async_copy(wg_hbm.at[e], wg_buf.at[s], wsem.at[s]),
                pltpu.make_async_copy(wu_hbm.at[e], wu_buf.at[s], wsem.at[s]),
                pltpu.make_async_copy(wd_hbm.at[e], wd_buf.at[s], wsem.at[s]))

    @pl.when(i == 0)
    def _():
        for cp in weight_copies(be_ref[0], 0):
            cp.start(priority=WEIGHT_DMA_PRIORITY)
        _row_gather(tok_ref, bm, ns, h_hbm, xbuf.at[0], sem.at[0])

    @pl.when((first_ref[i] == 1) & (nexte_ref[i] >= 0))
    def _():
        for cp in weight_copies(nexte_ref[i], 1 - wslot):
            cp.start(priority=WEIGHT_DMA_PRIORITY)

    @pl.when(i + 1 < nact)
    def _():
        _row_gather(tokn_ref, bm, ns, h_hbm, xbuf.at[1 - slot], sem.at[1 - slot])

    @pl.when(first_ref[i] == 1)
    def _():
        for cp in weight_copies(be_ref[i], wslot):
            cp.wait()

    @pl.when(i < nact)
    def _():
        _row_gather_wait(bm, ns, h_hbm, xbuf.at[slot], sem.at[slot])
        xb = jnp.concatenate(_slab_load(xbuf.at[slot], 0, bm, ns), axis=-1).astype(BF16)
        gate = _dot(xb, wg_buf[wslot].astype(BF16))
        up = _dot(xb, wu_buf[wslot].astype(BF16))
        hid = (_silu(gate) * up).astype(BF16)
        _slab_store(ys_ref, _dot(hid, wd_buf[wslot].astype(BF16)))

    @pl.when(i >= nact)
    def _():
        ys_ref[...] = jnp.zeros_like(ys_ref)


def _experts(h2_slab, block_e, nact, buf_tok, w_gate, w_up, w_down):
    E, D, De = w_gate.shape
    ns = D // LANES
    bm = MOE_BLOCK
    n_pad = buf_tok.shape[0]
    nb = n_pad // bm
    tok3 = buf_tok.reshape(nb, 1, bm)

    idx = jnp.arange(nb, dtype=jnp.int32)
    active = idx < nact[0]
    prev_e = jnp.concatenate([jnp.full((1,), -1, jnp.int32), block_e[:-1]])
    first = (active & (block_e != prev_e)).astype(jnp.int32)
    wslot = ((jnp.cumsum(first) - 1) % 2).astype(jnp.int32)
    later_first = (first[None, :] == 1) & (idx[None, :] > idx[:, None])
    nxt = jnp.min(jnp.where(later_first, idx[None, :], nb), axis=1)
    next_e = jnp.where(nxt < nb, _lookup(block_e, jnp.minimum(nxt, nb - 1)), -1).astype(jnp.int32)

    any_spec = pl.BlockSpec(memory_space=pl.ANY)
    grid_spec = pltpu.PrefetchScalarGridSpec(
        num_scalar_prefetch=5,
        grid=(nb,),
        in_specs=[pl.BlockSpec((1, 1, bm), lambda i, *_: (i, 0, 0), memory_space=pltpu.SMEM),
                  pl.BlockSpec((1, 1, bm), lambda i, *_: (jnp.minimum(i + 1, nb - 1), 0, 0),
                               memory_space=pltpu.SMEM),
                  any_spec, any_spec, any_spec, any_spec],
        out_specs=pl.BlockSpec((bm * ns, LANES), lambda i, *_: (i, 0)),
        scratch_shapes=[pltpu.VMEM((2, bm * ns, LANES), F32),
                        pltpu.VMEM((2, D, De), F32), pltpu.VMEM((2, D, De), F32),
                        pltpu.VMEM((2, De, D), F32),
                        pltpu.SemaphoreType.DMA((2,)), pltpu.SemaphoreType.DMA((2,))],
    )
    return pl.pallas_call(
        functools.partial(_expert_kernel, bm=bm, ns=ns),
        grid_spec=grid_spec,
        out_shape=jax.ShapeDtypeStruct((n_pad * ns, LANES), F32),
        compiler_params=_cparams("arbitrary"),
        name="experts",
    )(block_e, first, wslot, next_e, nact, tok3, tok3, h2_slab, w_gate, w_up, w_down)


def _combine_kernel(pos_ref, posn_ref, ys_hbm, x2_ref, mod_ref, route_ref, g_ref, o_ref,
                    ybuf, sem, *, tm, ns):
    i = pl.program_id(0)
    nsteps = pl.num_programs(0)
    slot = i % 2
    nrows = TOP_K * tm

    @pl.when(i == 0)
    def _():
        _row_gather(pos_ref, nrows, ns, ys_hbm, ybuf.at[0], sem.at[0])

    @pl.when(i + 1 < nsteps)
    def _():
        _row_gather(posn_ref, nrows, ns, ys_hbm, ybuf.at[1 - slot], sem.at[1 - slot])

    _row_gather_wait(nrows, ns, ys_hbm, ybuf.at[slot], sem.at[slot])
    route = route_ref[...]
    w0 = route[:, 2:3]
    w1 = route[:, 3:4]
    y0 = _slab_load(ybuf.at[slot], 0, tm, ns)
    y1 = _slab_load(ybuf.at[slot], tm, tm, ns)
    y = jnp.concatenate([w0 * a + w1 * b for a, b in zip(y0, y1)], axis=-1)
    x3 = x2_ref[...] + mod_ref[0, 5:6, :] * y
    ms = jnp.mean(x3 * x3, axis=-1, keepdims=True)
    o_ref[...] = x3 * lax.rsqrt(ms + EPS) * g_ref[...]


def _combine(ys_slab, pos, x2, mod3, route, norm_g, seq):
    T, D = x2.shape
    ns = D // LANES
    tm = 256
    nt = T // tm
    pos3 = pos.reshape(nt, tm, TOP_K).transpose(0, 2, 1).reshape(nt, 1, TOP_K * tm)
    row = lambda i: (i, 0)
    return pl.pallas_call(
        functools.partial(_combine_kernel, tm=tm, ns=ns),
        grid=(nt,),
        in_specs=[pl.BlockSpec((1, 1, TOP_K * tm), lambda i: (i, 0, 0), memory_space=pltpu.SMEM),
                  pl.BlockSpec((1, 1, TOP_K * tm), lambda i: (jnp.minimum(i + 1, nt - 1), 0, 0),
                               memory_space=pltpu.SMEM),
                  pl.BlockSpec(memory_space=pl.ANY),
                  pl.BlockSpec((tm, D), row),
                  pl.BlockSpec((1, 6, D), lambda i: (i * tm // seq, 0, 0)),
                  pl.BlockSpec((tm, LANES), row),
                  pl.BlockSpec((1, D), lambda i: (0, 0))],
        out_specs=pl.BlockSpec((tm, D), row),
        out_shape=jax.ShapeDtypeStruct((T, D), F32),
        scratch_shapes=[pltpu.VMEM((2, TOP_K * tm * ns, LANES), F32),
                        pltpu.SemaphoreType.DMA((2,))],
        compiler_params=_cparams("arbitrary"),
        name="combine",
    )(pos3, pos3, ys_slab, x2, mod3, route, norm_g.reshape(1, D))


def _lookup(table, idx):
    hit = idx[..., None] == jnp.arange(table.shape[0], dtype=idx.dtype)
    return jnp.sum(jnp.where(hit, table, 0), axis=-1)


def _dispatch_plan(route, n_tokens):
    A = n_tokens * TOP_K
    bm = MOE_BLOCK
    nb = (A + N_EXPERTS * (bm - 1)) // bm + 1
    flat_e = route[:, :TOP_K].astype(jnp.int32).reshape(A)
    iota = jnp.arange(A, dtype=jnp.int32)
    _, order = lax.sort_key_val(flat_e, iota)
    _, inv = lax.sort_key_val(order, iota)
    onehot = flat_e[:, None] == jnp.arange(N_EXPERTS, dtype=jnp.int32)[None, :]
    counts = lax.optimization_barrier(jnp.sum(onehot, axis=0, dtype=jnp.int32))
    padded = (counts + bm - 1) // bm * bm
    pad_end = jnp.cumsum(padded)
    pad_start = pad_end - padded
    start = jnp.cumsum(counts) - counts
    pos = inv + jnp.sum(jnp.where(onehot, (pad_start - start)[None, :], 0), axis=1)
    block_start = jnp.arange(nb, dtype=jnp.int32) * bm
    block_e = jnp.minimum(jnp.sum(pad_end[None, :] <= block_start[:, None], axis=1),
                          N_EXPERTS - 1).astype(jnp.int32)
    r = (block_start - _lookup(pad_start, block_e))[:, None] + jnp.arange(bm, dtype=jnp.int32)[None, :]
    src = jnp.clip(_lookup(start, block_e)[:, None] + r, 0, A - 1)
    buf_tok = jnp.where(r < _lookup(counts, block_e)[:, None], order[src] // TOP_K, 0)
    nact = (pad_end[-1:] // bm).astype(jnp.int32)
    return buf_tok.reshape(nb * bm), pos.astype(jnp.int32), block_e, nact


def kernel(x, c, w_ada, b_ada, norm1_g, w_in, pool_w, pool_scale, conv_w, a_log, dt_bias,
           o_norm_g, w_out, norm2_g, w_router_group, b_router_group, w_router_expert,
           b_router_expert, w_gate, w_up, w_down, norm_f_g):
    B, S, D = x.shape
    T = B * S
    depth = w_ada.shape[0]
    pool_width = pool_w.shape[1] * pool_w.shape[2]
    n_main = pool_width + 4 * DN_HEADS * DN_HEAD_DIM
    n_route = N_GROUPS + N_EXPERTS

    assert depth == 1, "kernel supports the single-layer configuration only"
    l = 0
    xt = x.reshape(T, D)
    mod3 = _ada(c, w_ada[l], b_ada[l]).reshape(B, 6, D)

    w_all = w_in[l].astype(BF16)
    w_ba = jnp.pad(w_in[l, :, n_main:], ((0, 0), (0, LANES - 2 * DN_HEADS))).astype(BF16)
    gate_pad = (DN_HEADS, LANES - 2 * DN_HEADS)
    alog_lanes = jnp.pad(a_log[l], gate_pad).reshape(1, LANES)
    dtb_lanes = jnp.pad(dt_bias[l], gate_pad).reshape(1, LANES)
    proj, gates = _inproj(xt, mod3, norm1_g[l], w_all, n_main, w_ba, alog_lanes, dtb_lanes, S)

    y_pool = _pool(proj, pool_w[l], pool_scale[l], B, S)
    y_dn = _deltanet(proj, gates, conv_w[l], o_norm_g[l], B, S, pool_width)

    w_router = jnp.pad(jnp.concatenate([w_router_group[l], w_router_expert[l]], axis=1),
                       ((0, 0), (0, LANES - n_route)))
    b_router = jnp.pad(jnp.concatenate([b_router_group[l], b_router_expert[l]]),
                       (0, LANES - n_route)).reshape(1, LANES)
    w_router_hi = w_router.astype(BF16)
    w_router_lo = (w_router - w_router_hi.astype(F32)).astype(BF16)
    x2, h2, route = _outproj(y_pool, y_dn, w_out[l].astype(BF16), xt, mod3, norm2_g[l],
                             jnp.concatenate([w_router_hi, w_router_lo], axis=1), b_router, S)

    buf_tok, pos, block_e, nact = _dispatch_plan(route, T)
    ys = _experts(h2, block_e, nact, buf_tok, w_gate[l], w_up[l], w_down[l])
    out = _combine(ys, pos, x2, mod3, route, norm_f_g, S)
    return out.reshape(B, S, D)
```

```python
import functools

import jax
import jax.numpy as jnp
from jax import lax
from jax.experimental import pallas as pl
from jax.experimental.pallas import tpu as pltpu

F32 = jnp.float32
BF16 = jnp.bfloat16

POOL_GROUPS = 4
POOL_WINDOWS = (2, 4, 8, 16)
POOL_HALO = 16
DN_HEADS = 8
DN_HEAD_DIM = 128
CONV_WIDTH = 4
DN_CHUNK = 128
N_GROUPS = 4
EXPERTS_PER_GROUP = 8
N_EXPERTS = N_GROUPS * EXPERTS_PER_GROUP
TOP_K = 2
MOE_BLOCK = 256
EPS = 1e-6
LANES = 128
NEG_BIG = -3.0e38
VMEM_LIMIT = 60 * 1024 * 1024


def _silu(x):
    half = 0.5 * x
    return half * (1.0 + jnp.tanh(half))


def _chunk_cumsum(x, chunk):
    rmod = lax.broadcasted_iota(jnp.int32, x.shape, 0) & (chunk - 1)
    k = 1
    while k < chunk:
        x = x + jnp.where(rmod >= k, pltpu.roll(x, k, axis=0), 0.0)
        k *= 2
    return x


def _dot(a, b):
    return jnp.dot(a, b, preferred_element_type=F32)


def _cparams(*sem):
    return pltpu.CompilerParams(dimension_semantics=sem, vmem_limit_bytes=VMEM_LIMIT)


def _ada_kernel(c_ref, w_ref, b_ref, o_ref):
    ca = _silu(c_ref[...])
    nb = ca.shape[0]
    c_hi = ca.astype(BF16).astype(F32)
    hi_lo = jnp.concatenate([c_hi, ca - c_hi], axis=0).astype(BF16)
    w = w_ref[...]
    w_hi = w.astype(BF16)
    w_lo = (w - w_hi.astype(F32)).astype(BF16)
    both = _dot(hi_lo, w_hi)
    o_ref[...] = both[:nb] + (both[nb:] + _dot(c_hi.astype(BF16), w_lo)) + b_ref[...]


def _ada(c, w_ada, b_ada):
    B, D = c.shape
    N = w_ada.shape[1]
    tn = 1024
    return pl.pallas_call(
        _ada_kernel,
        grid=(N // tn,),
        in_specs=[pl.BlockSpec((B, D), lambda j: (0, 0)),
                  pl.BlockSpec((D, tn), lambda j: (0, j)),
                  pl.BlockSpec((1, tn), lambda j: (0, j))],
        out_specs=pl.BlockSpec((B, tn), lambda j: (0, j)),
        out_shape=jax.ShapeDtypeStruct((B, N), F32),
        compiler_params=_cparams("arbitrary"),
        name="ada",
    )(c, w_ada, b_ada.reshape(1, N))


def _pool_mix(u, halo, pos0, w_ref, sc_ref, o_ref, gd):
    rows = u.shape[0]
    tpos = lax.broadcasted_iota(jnp.int32, (rows, 1), 0) + pos0 + 1
    for g in range(POOL_GROUPS):
        win = POOL_WINDOWS[g]
        cols = slice(g * gd, (g + 1) * gd)
        s = jnp.concatenate([halo[:, cols], u[:, cols]], axis=0)
        k = 1
        while k < win:
            s = s + pltpu.roll(s, k, axis=0)
            k *= 2
        cnt = jnp.minimum(tpos, win).astype(F32)
        diff = s[POOL_HALO:, :] / cnt - u[:, cols]
        y = _dot(diff.astype(BF16), w_ref[g].astype(BF16)) * sc_ref[:, cols]
        o_ref[:, cols] = y.astype(o_ref.dtype)


def _inproj_kernel(x_ref, mod_ref, g_ref, w_ref, wba_ref, alog_ref, dtb_ref, pw_ref, psc_ref,
                   proj_ref, gates_ref, ypool_ref, h_scr, u_scr, halo_scr, *, seq, gd):
    i = pl.program_id(0)
    j = pl.program_id(1)
    ni = pl.num_programs(0)
    nj = pl.num_programs(1)
    tm = x_ref.shape[0]
    cur = i % 2

    def prologue():
        x = x_ref[...]
        ms = jnp.mean(x * x, axis=-1, keepdims=True)
        y = x * lax.rsqrt(ms + EPS) * g_ref[...]
        h = (y * (1.0 + mod_ref[0, 1:2, :]) + mod_ref[0, 0:1, :]).astype(BF16)
        ba = _dot(h, wba_ref[...])
        lane = lax.broadcasted_iota(jnp.int32, ba.shape, 1)
        a = ba + dtb_ref[...]
        softplus = jnp.maximum(a, 0.0) + jnp.log1p(jnp.exp(-jnp.abs(a)))
        gc = _chunk_cumsum(-jnp.exp(alog_ref[...]) * softplus, DN_CHUNK)
        gates_ref[...] = jnp.where(lane < DN_HEADS, jax.nn.sigmoid(ba), gc)
        return h

    @pl.when((i == 0) & (j == 0))
    def _():
        h = prologue()
        h_scr[0] = h
        halo_scr[...] = jnp.zeros_like(halo_scr)
        u_scr[...] = _dot(h, w_ref[...])

    @pl.when((i > 0) & (j == 0))
    def _():
        u_scr[...] = _dot(h_scr[cur], w_ref[...])

    @pl.when(j == 1)
    def _():
        proj_ref[...] = _dot(h_scr[cur], w_ref[...])
        pos0 = (i * tm) % seq
        u = u_scr[...]
        halo = jnp.where(pos0 > 0, halo_scr[...], 0.0)
        _pool_mix(u, halo, pos0, pw_ref, psc_ref, ypool_ref, gd)
        halo_scr[...] = u[tm - POOL_HALO:, :]

    @pl.when((j > 1) & (j < nj - 1))
    def _():
        proj_ref[...] = _dot(h_scr[cur], w_ref[...])

    @pl.when((j == nj - 1) & (i + 1 < ni))
    def _():
        proj_ref[...] = _dot(h_scr[cur], w_ref[...])
        h_scr[1 - cur] = prologue()

    @pl.when((j == nj - 1) & (i + 1 == ni))
    def _():
        proj_ref[...] = _dot(h_scr[cur], w_ref[...])


def _inproj(x2d, mod3, norm_g, w_all, n_main, w_ba, alog_lanes, dtb_lanes, pool_w, pool_scale, seq):
    T, D = x2d.shape
    G, gd, _ = pool_w.shape
    tm, tn = 1024, 1024
    ni, nj = T // tm, n_main // tn
    assert tm % DN_CHUNK == 0 and seq % tm == 0 and n_main % tn == 0 and G * gd == tn and nj >= 3

    def ahead(i, j):
        return jnp.minimum(i + (j == nj - 1).astype(jnp.int32), ni - 1)

    const = lambda i, j: (0, 0)
    return pl.pallas_call(
        functools.partial(_inproj_kernel, seq=seq, gd=gd),
        grid=(ni, nj),
        in_specs=[pl.BlockSpec((tm, D), lambda i, j: (ahead(i, j), 0)),
                  pl.BlockSpec((1, 6, D), lambda i, j: (ahead(i, j) * tm // seq, 0, 0)),
                  pl.BlockSpec((1, D), const),
                  pl.BlockSpec((D, tn), lambda i, j: (0, j)),
                  pl.BlockSpec((D, LANES), const),
                  pl.BlockSpec((1, LANES), const),
                  pl.BlockSpec((1, LANES), const),
                  pl.BlockSpec((G, gd, gd), lambda i, j: (0, 0, 0)),
                  pl.BlockSpec((1, tn), const)],
        out_specs=[pl.BlockSpec((tm, tn), lambda i, j: (i, jnp.maximum(j - 1, 0))),
                   pl.BlockSpec((tm, LANES), lambda i, j: (ahead(i, j), 0)),
                   pl.BlockSpec((tm, tn), lambda i, j: (i, 0))],
        out_shape=[jax.ShapeDtypeStruct((T, n_main - tn), F32),
                   jax.ShapeDtypeStruct((T, LANES), F32),
                   jax.ShapeDtypeStruct((T, tn), BF16)],
        scratch_shapes=[pltpu.VMEM((2, tm, D), BF16), pltpu.VMEM((tm, tn), F32),
                        pltpu.VMEM((POOL_HALO, tn), F32)],
        compiler_params=_cparams("arbitrary", "arbitrary"),
        name="inproj",
    )(x2d, mod3, norm_g.reshape(1, D), w_all, w_ba, alog_lanes, dtb_lanes, pool_w,
      pool_scale.reshape(1, tn))


def _bmm(a, b):
    return jnp.einsum('nij,njk->nik', a.astype(BF16), b.astype(BF16),
                      preferred_element_type=F32)


def _bmm_nt(a, b):
    return jnp.einsum('nid,njd->nij', a.astype(BF16), b.astype(BF16),
                      preferred_element_type=F32)


def _dn_prepare(q_raw, k_raw, v_raw, cwq, cwk, cwv, beta_col, gc_col, *, chunk):
    seq, d = q_raw.shape
    n = seq // chunk
    top = lax.broadcasted_iota(jnp.int32, (8, d), 0)

    def shift(a, sh):
        r = pltpu.roll(a, sh, axis=0)
        return jnp.concatenate([jnp.where(top >= sh, r[:8], 0.0), r[8:]], axis=0)

    def conv_silu(x, cw):
        assert CONV_WIDTH == 4
        w0, w1, w2, w3 = (cw[j:j + 1, :] for j in range(CONV_WIDTH))
        x1 = shift(x, 1)
        return _silu(x * w3 + x1 * w2 + shift(x * w1 + x1 * w0, 2))

    def l2n(x):
        return x * lax.rsqrt(jnp.sum(x * x, axis=-1, keepdims=True) + EPS)

    qn = l2n(conv_silu(q_raw, cwq)) * (d ** -0.5)
    kn = l2n(conv_silu(k_raw, cwk))
    v = conv_silu(v_raw, cwv)
    beta = jnp.broadcast_to(beta_col, (seq, d))
    gc = jnp.broadcast_to(gc_col, (seq, d))

    gc3 = gc.reshape(n, chunk, d)
    ii = lax.broadcasted_iota(jnp.int32, (chunk, chunk), 0)
    jj = lax.broadcasted_iota(jnp.int32, (chunk, chunk), 1)
    incl = (ii >= jj)[None]
    strict = (ii > jj)[None]
    diff = gc3 - jnp.swapaxes(gc3, 1, 2)
    decay = jnp.where(incl, jnp.exp(jnp.where(incl, diff, 0.0)), 0.0)

    kb = kn * beta
    kn3 = kn.reshape(n, chunk, d)
    lmat = jnp.where(strict, _bmm_nt(kb.reshape(n, chunk, d), kn3) * decay, 0.0)
    qk = jnp.where(incl, _bmm_nt(qn.reshape(n, chunk, d), kn3) * decay, 0.0)

    def merge_mask(lv):
        same = (ii >> (lv + 1)) == (jj >> (lv + 1))
        return (same & (((ii >> lv) & 1) == 1) & (((jj >> lv) & 1) == 0))[None]

    tinv = (ii == jj).astype(F32)[None] - jnp.where(merge_mask(0), lmat, 0.0)
    for lv in range(1, chunk.bit_length() - 1):
        a21 = jnp.where(merge_mask(lv), lmat, 0.0)
        tinv = tinv - _bmm(tinv, _bmm(a21, tinv))

    egc = jnp.exp(gc)
    rhs = jnp.concatenate([v * beta, kb * egc], axis=-1).reshape(n, chunk, 2 * d)
    uw = _bmm(tinv, rhs)
    glast = gc3[:, chunk - 1:chunk, :]
    kdt = jnp.swapaxes(kn3 * jnp.exp(glast - gc3), 1, 2)
    kuw = _bmm(kdt, uw)
    quw = _bmm(qk, uw)
    qp = (qn * egc).reshape(n, chunk, d) - quw[..., d:]
    return quw[..., :d], qp, kuw[..., :d], kuw[..., d:], jnp.exp(glast)


def _dn_kernel(q_ref, k_ref, v_ref, z_ref, gates_ref, cwq_ref, cwk_ref, cwv_ref, on_ref,
               o_ref, o_scr, *, chunk, heads_per_step):
    d = DN_HEAD_DIM
    seq = q_ref.shape[0]
    n = seq // chunk
    lane = lax.broadcasted_iota(jnp.int32, (seq, LANES), 1)
    gates = gates_ref[...]
    prepared = []
    for i in range(heads_per_step):
        head = pl.program_id(1) * heads_per_step + i
        cols = slice(i * d, (i + 1) * d)
        beta_col = jnp.sum(jnp.where(lane == head, gates, 0.0), axis=-1, keepdims=True)
        gc_col = jnp.sum(jnp.where(lane == head + DN_HEADS, gates, 0.0), axis=-1, keepdims=True)
        prepared.append(_dn_prepare(q_ref[:, cols], k_ref[:, cols], v_ref[:, cols],
                                    cwq_ref[:, cols], cwk_ref[:, cols], cwv_ref[:, cols],
                                    beta_col, gc_col, chunk=chunk))

    states = [jnp.zeros((d, d), F32) for _ in range(heads_per_step)]
    for c in range(n):
        for i, (o0, qp, kub, kuw, eglast) in enumerate(prepared):
            sb = states[i].astype(BF16)
            o_scr[c * chunk:(c + 1) * chunk, i * d:(i + 1) * d] = o0[c] + _dot(qp[c].astype(BF16), sb)
            states[i] = states[i] * eglast[c] + kub[c] - _dot(kuw[c].astype(BF16), sb)

    for i in range(heads_per_step):
        cols = slice(i * d, (i + 1) * d)
        o = o_scr[:, cols]
        o = o * lax.rsqrt(jnp.mean(o * o, axis=-1, keepdims=True) + EPS) * on_ref[...]
        o_ref[:, cols] = (o * _silu(z_ref[:, cols])).astype(o_ref.dtype)


def _deltanet(proj, gates, conv_w, o_norm_g, batch, seq, col0):
    T = proj.shape[0]
    d = DN_HEAD_DIM
    H = DN_HEADS
    hps = 2
    w = hps * d
    cb = col0 // w
    nhb = H // hps

    def colspec(off):
        return pl.BlockSpec((seq, w), lambda b, h: (b, off + h))

    def convspec(off):
        return pl.BlockSpec((CONV_WIDTH, w), lambda b, h: (0, off + h))

    return pl.pallas_call(
        functools.partial(_dn_kernel, chunk=DN_CHUNK, heads_per_step=hps),
        grid=(batch, nhb),
        in_specs=[colspec(cb), colspec(cb + nhb), colspec(cb + 2 * nhb), colspec(cb + 3 * nhb),
                  pl.BlockSpec((seq, LANES), lambda b, h: (b, 0)),
                  convspec(0), convspec(nhb), convspec(2 * nhb),
                  pl.BlockSpec((1, d), lambda b, h: (0, 0))],
        out_specs=pl.BlockSpec((seq, w), lambda b, h: (b, h)),
        out_shape=jax.ShapeDtypeStruct((T, H * d), BF16),
        scratch_shapes=[pltpu.VMEM((seq, w), F32)],
        compiler_params=_cparams("arbitrary", "arbitrary"),
        name="deltanet",
    )(proj, proj, proj, proj, gates, conv_w, conv_w, conv_w, o_norm_g.reshape(1, d))


def _route(logits):
    lane = lax.broadcasted_iota(jnp.int32, logits.shape, 1)
    lanef = lane.astype(F32)
    far = float(LANES)

    def first_max(vals):
        m = jnp.max(vals, axis=-1, keepdims=True)
        idx = jnp.min(jnp.where(vals == m, lanef, far), axis=-1, keepdims=True)
        return m, idx

    gl = jnp.where(lane < N_GROUPS, logits, NEG_BIG)
    gmax, gidx = first_max(gl)
    p_top = 1.0 / jnp.sum(jnp.exp(gl - gmax), axis=-1, keepdims=True)
    lo = N_GROUPS + EXPERTS_PER_GROUP * gidx
    el = jnp.where(lanef >= lo, jnp.where(lanef < lo + EXPERTS_PER_GROUP, logits, NEG_BIG), NEG_BIG)
    m1, i1 = first_max(el)
    m2, i2 = first_max(jnp.where(lanef == i1, NEG_BIG, el))
    t = jnp.exp(m2 - m1)
    w1 = p_top / (1.0 + t)
    w2 = w1 * t
    return jnp.where(lane == 0, i1 - N_GROUPS,
                     jnp.where(lane == 1, i2 - N_GROUPS,
                               jnp.where(lane == 2, w1, jnp.where(lane == 3, w2, 0.0))))


def _slab_store(ref, val):
    rows, width = val.shape
    ns = width // LANES
    for s in range(ns):
        ref[pl.ds(s, rows, stride=ns), :] = val[:, s * LANES:(s + 1) * LANES]


def _slab_load(ref, first_row, rows, ns):
    return [ref[pl.ds(first_row * ns + s, rows, stride=ns), :] for s in range(ns)]


def _outproj_kernel(yp_ref, yd_ref, wo_ref, x_ref, mod_ref, g_ref, wr_ref, br_ref,
                    x2_ref, h2_ref, route_ref):
    half = yp_ref.shape[1]
    out = _dot(yp_ref[...], wo_ref[:half, :]) + _dot(yd_ref[...], wo_ref[half:, :])
    x2 = x_ref[...] + mod_ref[0, 2:3, :] * out
    x2_ref[...] = x2
    ms = jnp.mean(x2 * x2, axis=-1, keepdims=True)
    y = x2 * lax.rsqrt(ms + EPS) * g_ref[...]
    h2 = y * (1.0 + mod_ref[0, 4:5, :]) + mod_ref[0, 3:4, :]
    _slab_store(h2_ref, h2)
    hi = h2.astype(BF16)
    lo = (h2 - hi.astype(F32)).astype(BF16)
    both = _dot(hi, wr_ref[...])
    logits = both[:, :LANES] + (_dot(lo, wr_ref[:, :LANES]) + both[:, LANES:]) + br_ref[...]
    route_ref[...] = _route(logits)


def _outproj(y_pool, y_dn, w_out_bf, x2d, mod3, norm_g, w_router, b_router, seq):
    T, D = x2d.shape
    half = y_pool.shape[1]
    tm = 256
    row = lambda i: (i, 0)
    const = lambda i: (0, 0)
    return pl.pallas_call(
        _outproj_kernel,
        grid=(T // tm,),
        in_specs=[pl.BlockSpec((tm, half), row),
                  pl.BlockSpec((tm, half), row),
                  pl.BlockSpec((2 * half, D), const),
                  pl.BlockSpec((tm, D), row),
                  pl.BlockSpec((1, 6, D), lambda i: (i * tm // seq, 0, 0)),
                  pl.BlockSpec((1, D), const),
                  pl.BlockSpec((D, 2 * LANES), const),
                  pl.BlockSpec((1, LANES), const)],
        out_specs=[pl.BlockSpec((tm, D), row),
                   pl.BlockSpec((tm * (D // LANES), LANES), row),
                   pl.BlockSpec((tm, LANES), row)],
        out_shape=[jax.ShapeDtypeStruct((T, D), F32),
                   jax.ShapeDtypeStruct((T * (D // LANES), LANES), F32),
                   jax.ShapeDtypeStruct((T, LANES), F32)],
        compiler_params=_cparams("arbitrary"),
        name="outproj",
    )(y_pool, y_dn, w_out_bf, x2d, mod3, norm_g.reshape(1, D), w_router, b_router)


GATHER_UNROLL = 8


def _row_gather(idx_ref, nrows, ns, src_hbm, dst, sem):
    def body(r, carry):
        src_row = pl.multiple_of(idx_ref[0, 0, r] * ns, ns)
        dst_row = pl.multiple_of(r * ns, ns)
        pltpu.make_async_copy(src_hbm.at[pl.ds(src_row, ns), :], dst.at[pl.ds(dst_row, ns), :],
                              sem).start()
        return carry
    lax.fori_loop(0, nrows, body, 0, unroll=GATHER_UNROLL)


def _row_gather_wait(nrows, ns, src_hbm, dst, sem):
    pltpu.make_async_copy(src_hbm.at[pl.ds(0, nrows * ns), :], dst, sem).wait()


WEIGHT_DMA_PRIORITY = 1


def _expert_kernel(be_ref, first_ref, wslot_ref, nexte_ref, nact_ref, tok_ref, tokn_ref,
                   h_hbm, wg_hbm, wu_hbm, wd_hbm, ys_ref, xbuf, wg_buf, wu_buf, wd_buf,
                   wg_bf, wu_bf, wd_bf, sem, wsem, *, bm, ns):
    i = pl.program_id(0)
    nact = nact_ref[0]
    slot = i % 2
    wslot = wslot_ref[i]

    def weight_copies(e, s):
        return (pltpu.make_async_copy(wg_hbm.at[e], wg_buf.at[s], wsem.at[s]),
                pltpu.make_async_copy(wu_hbm.at[e], wu_buf.at[s], wsem.at[s]),
                pltpu.make_async_copy(wd_hbm.at[e], wd_buf.at[s], wsem.at[s]))

    @pl.when(i == 0)
    def _():
        for cp in weight_copies(be_ref[0], 0):
            cp.start(priority=WEIGHT_DMA_PRIORITY)
        _row_gather(tok_ref, bm, ns, h_hbm, xbuf.at[0], sem.at[0])

    @pl.when((first_ref[i] == 1) & (nexte_ref[i] >= 0))
    def _():
        for cp in weight_copies(nexte_ref[i], 1 - wslot):
            cp.start(priority=WEIGHT_DMA_PRIORITY)

    @pl.when(i + 1 < nact)
    def _():
        _row_gather(tokn_ref, bm, ns, h_hbm, xbuf.at[1 - slot], sem.at[1 - slot])

    @pl.when(first_ref[i] == 1)
    def _():
        for cp in weight_copies(be_ref[i], wslot):
            cp.wait()
        wg_bf[...] = wg_buf[wslot].astype(BF16)
        wu_bf[...] = wu_buf[wslot].astype(BF16)
        wd_bf[...] = wd_buf[wslot].astype(BF16)

    @pl.when(i < nact)
    def _():
        _row_gather_wait(bm, ns, h_hbm, xbuf.at[slot], sem.at[slot])
        xb = jnp.concatenate(_slab_load(xbuf.at[slot], 0, bm, ns), axis=-1).astype(BF16)
        gate = _dot(xb, wg_bf[...])
        up = _dot(xb, wu_bf[...])
        hid = (_silu(gate) * up).astype(BF16)
        _slab_store(ys_ref, _dot(hid, wd_bf[...]))

    @pl.when(i >= nact)
    def _():
        ys_ref[...] = jnp.zeros_like(ys_ref)


def _experts(h2_slab, block_e, nact, buf_tok, w_gate, w_up, w_down):
    E, D, De = w_gate.shape
    ns = D // LANES
    bm = MOE_BLOCK
    n_pad = buf_tok.shape[0]
    nb = n_pad // bm
    tok3 = buf_tok.reshape(nb, 1, bm)

    idx = jnp.arange(nb, dtype=jnp.int32)
    active = idx < nact[0]
    prev_e = jnp.concatenate([jnp.full((1,), -1, jnp.int32), block_e[:-1]])
    first = (active & (block_e != prev_e)).astype(jnp.int32)
    wslot = ((jnp.cumsum(first) - 1) % 2).astype(jnp.int32)
    later_first = (first[None, :] == 1) & (idx[None, :] > idx[:, None])
    nxt = jnp.min(jnp.where(later_first, idx[None, :], nb), axis=1)
    next_e = jnp.where(nxt < nb, _lookup(block_e, jnp.minimum(nxt, nb - 1)), -1).astype(jnp.int32)

    any_spec = pl.BlockSpec(memory_space=pl.ANY)
    grid_spec = pltpu.PrefetchScalarGridSpec(
        num_scalar_prefetch=5,
        grid=(nb,),
        in_specs=[pl.BlockSpec((1, 1, bm), lambda i, *_: (i, 0, 0), memory_space=pltpu.SMEM),
                  pl.BlockSpec((1, 1, bm), lambda i, *_: (jnp.minimum(i + 1, nb - 1), 0, 0),
                               memory_space=pltpu.SMEM),
                  any_spec, any_spec, any_spec, any_spec],
        out_specs=pl.BlockSpec((bm * ns, LANES), lambda i, *_: (i, 0)),
        scratch_shapes=[pltpu.VMEM((2, bm * ns, LANES), F32),
                        pltpu.VMEM((2, D, De), F32), pltpu.VMEM((2, D, De), F32),
                        pltpu.VMEM((2, De, D), F32),
                        pltpu.VMEM((D, De), BF16), pltpu.VMEM((D, De), BF16),
                        pltpu.VMEM((De, D), BF16),
                        pltpu.SemaphoreType.DMA((2,)), pltpu.SemaphoreType.DMA((2,))],
    )
    return pl.pallas_call(
        functools.partial(_expert_kernel, bm=bm, ns=ns),
        grid_spec=grid_spec,
        out_shape=jax.ShapeDtypeStruct((n_pad * ns, LANES), F32),
        compiler_params=_cparams("arbitrary"),
        name="experts",
    )(block_e, first, wslot, next_e, nact, tok3, tok3, h2_slab, w_gate, w_up, w_down)


def _combine_kernel(pos_ref, posn_ref, ys_hbm, x2_ref, mod_ref, route_ref, g_ref, o_ref,
                    ybuf, sem, *, tm, ns):
    i = pl.program_id(0)
    nsteps = pl.num_programs(0)
    slot = i % 2
    nrows = TOP_K * tm

    @pl.when(i == 0)
    def _():
        _row_gather(pos_ref, nrows, ns, ys_hbm, ybuf.at[0], sem.at[0])

    @pl.when(i + 1 < nsteps)
    def _():
        _row_gather(posn_ref, nrows, ns, ys_hbm, ybuf.at[1 - slot], sem.at[1 - slot])

    _row_gather_wait(nrows, ns, ys_hbm, ybuf.at[slot], sem.at[slot])
    route = route_ref[...]
    w0 = route[:, 2:3]
    w1 = route[:, 3:4]
    y0 = _slab_load(ybuf.at[slot], 0, tm, ns)
    y1 = _slab_load(ybuf.at[slot], tm, tm, ns)
    y = jnp.concatenate([w0 * a + w1 * b for a, b in zip(y0, y1)], axis=-1)
    x3 = x2_ref[...] + mod_ref[0, 5:6, :] * y
    ms = jnp.mean(x3 * x3, axis=-1, keepdims=True)
    o_ref[...] = x3 * lax.rsqrt(ms + EPS) * g_ref[...]


def _combine(ys_slab, pos, x2, mod3, route, norm_g, seq):
    T, D = x2.shape
    ns = D // LANES
    tm = 256
    nt = T // tm
    pos3 = pos.reshape(nt, tm, TOP_K).transpose(0, 2, 1).reshape(nt, 1, TOP_K * tm)
    row = lambda i: (i, 0)
    return pl.pallas_call(
        functools.partial(_combine_kernel, tm=tm, ns=ns),
        grid=(nt,),
        in_specs=[pl.BlockSpec((1, 1, TOP_K * tm), lambda i: (i, 0, 0), memory_space=pltpu.SMEM),
                  pl.BlockSpec((1, 1, TOP_K * tm), lambda i: (jnp.minimum(i + 1, nt - 1), 0, 0),
                               memory_space=pltpu.SMEM),
                  pl.BlockSpec(memory_space=pl.ANY),
                  pl.BlockSpec((tm, D), row),
                  pl.BlockSpec((1, 6, D), lambda i: (i * tm // seq, 0, 0)),
                  pl.BlockSpec((tm, LANES), row),
                  pl.BlockSpec((1, D), lambda i: (0, 0))],
        out_specs=pl.BlockSpec((tm, D), row),
        out_shape=jax.ShapeDtypeStruct((T, D), F32),
        scratch_shapes=[pltpu.VMEM((2, TOP_K * tm * ns, LANES), F32),
                        pltpu.SemaphoreType.DMA((2,))],
        compiler_params=_cparams("arbitrary"),
        name="combine",
    )(pos3, pos3, ys_slab, x2, mod3, route, norm_g.reshape(1, D))


def _lookup(table, idx):
    hit = idx[..., None] == jnp.arange(table.shape[0], dtype=idx.dtype)
    return jnp.sum(jnp.where(hit, table, 0), axis=-1)


def _dispatch_plan(route, n_tokens):
    A = n_tokens * TOP_K
    bm = MOE_BLOCK
    nb = (A + N_EXPERTS * (bm - 1)) // bm + 1
    flat_e = route[:, :TOP_K].astype(jnp.int32).reshape(A)
    iota = jnp.arange(A, dtype=jnp.int32)
    _, order = lax.sort_key_val(flat_e, iota)
    _, inv = lax.sort_key_val(order, iota)
    onehot = flat_e[:, None] == jnp.arange(N_EXPERTS, dtype=jnp.int32)[None, :]
    counts = lax.optimization_barrier(jnp.sum(onehot, axis=0, dtype=jnp.int32))
    padded = (counts + bm - 1) // bm * bm
    pad_end = jnp.cumsum(padded)
    pad_start = pad_end - padded
    start = jnp.cumsum(counts) - counts
    pos = inv + jnp.sum(jnp.where(onehot, (pad_start - start)[None, :], 0), axis=1)
    block_start = jnp.arange(nb, dtype=jnp.int32) * bm
    block_e = jnp.minimum(jnp.sum(pad_end[None, :] <= block_start[:, None], axis=1),
                          N_EXPERTS - 1).astype(jnp.int32)
    r = (block_start - _lookup(pad_start, block_e))[:, None] + jnp.arange(bm, dtype=jnp.int32)[None, :]
    src = jnp.clip(_lookup(start, block_e)[:, None] + r, 0, A - 1)
    buf_tok = jnp.where(r < _lookup(counts, block_e)[:, None], order[src] // TOP_K, 0)
    nact = (pad_end[-1:] // bm).astype(jnp.int32)
    return buf_tok.reshape(nb * bm), pos.astype(jnp.int32), block_e, nact


def kernel(x, c, w_ada, b_ada, norm1_g, w_in, pool_w, pool_scale, conv_w, a_log, dt_bias,
           o_norm_g, w_out, norm2_g, w_router_group, b_router_group, w_router_expert,
           b_router_expert, w_gate, w_up, w_down, norm_f_g):
    B, S, D = x.shape
    T = B * S
    depth = w_ada.shape[0]
    pool_width = pool_w.shape[1] * pool_w.shape[2]
    n_main = pool_width + 4 * DN_HEADS * DN_HEAD_DIM
    n_route = N_GROUPS + N_EXPERTS

    assert depth == 1, "kernel supports the single-layer configuration only"
    l = 0
    xt = x.reshape(T, D)
    mod3 = _ada(c, w_ada[l], b_ada[l]).reshape(B, 6, D)

    w_all = w_in[l].astype(BF16)
    w_ba = jnp.pad(w_in[l, :, n_main:], ((0, 0), (0, LANES - 2 * DN_HEADS))).astype(BF16)
    gate_pad = (DN_HEADS, LANES - 2 * DN_HEADS)
    alog_lanes = jnp.pad(a_log[l], gate_pad).reshape(1, LANES)
    dtb_lanes = jnp.pad(dt_bias[l], gate_pad).reshape(1, LANES)
    proj, gates, y_pool = _inproj(xt, mod3, norm1_g[l], w_all, n_main, w_ba, alog_lanes, dtb_lanes,
                                  pool_w[l], pool_scale[l], S)
    y_dn = _deltanet(proj, gates, conv_w[l], o_norm_g[l], B, S, 0)

    w_router = jnp.pad(jnp.concatenate([w_router_group[l], w_router_expert[l]], axis=1),
                       ((0, 0), (0, LANES - n_route)))
    b_router = jnp.pad(jnp.concatenate([b_router_group[l], b_router_expert[l]]),
                       (0, LANES - n_route)).reshape(1, LANES)
    w_router_hi = w_router.astype(BF16)
    w_router_lo = (w_router - w_router_hi.astype(F32)).astype(BF16)
    x2, h2, route = _outproj(y_pool, y_dn, w_out[l].astype(BF16), xt, mod3, norm2_g[l],
                             jnp.concatenate([w_router_hi, w_router_lo], axis=1), b_router, S)

    buf_tok, pos, block_e, nact = _dispatch_plan(route, T)
    ys = _experts(h2, block_e, nact, buf_tok, w_gate[l], w_up[l], w_down[l])
    out = _combine(ys, pos, x2, mod3, route, norm_f_g, S)
    return out.reshape(B, S, D)
```

```python
import functools

import jax
import jax.numpy as jnp
from jax import lax
from jax.experimental import pallas as pl
from jax.experimental.pallas import tpu as pltpu

F32 = jnp.float32
BF16 = jnp.bfloat16

POOL_GROUPS = 4
POOL_WINDOWS = (2, 4, 8, 16)
POOL_HALO = 16
DN_HEADS = 8
DN_HEAD_DIM = 128
CONV_WIDTH = 4
DN_CHUNK = 128
N_GROUPS = 4
EXPERTS_PER_GROUP = 8
N_EXPERTS = N_GROUPS * EXPERTS_PER_GROUP
TOP_K = 2
MOE_BLOCK = 256
EPS = 1e-6
LANES = 128
NEG_BIG = -3.0e38
VMEM_LIMIT = 60 * 1024 * 1024


def _silu(x):
    half = 0.5 * x
    return half * (1.0 + jnp.tanh(half))


def _chunk_cumsum(x, chunk):
    rmod = lax.broadcasted_iota(jnp.int32, x.shape, 0) & (chunk - 1)
    k = 1
    while k < chunk:
        x = x + jnp.where(rmod >= k, pltpu.roll(x, k, axis=0), 0.0)
        k *= 2
    return x


def _dot(a, b):
    return jnp.dot(a, b, preferred_element_type=F32)


def _cparams(*sem):
    return pltpu.CompilerParams(dimension_semantics=sem, vmem_limit_bytes=VMEM_LIMIT)


def _ada_kernel(c_ref, w_ref, b_ref, o_ref):
    ca = _silu(c_ref[...])
    nb = ca.shape[0]
    c_hi = ca.astype(BF16).astype(F32)
    hi_lo = jnp.concatenate([c_hi, ca - c_hi], axis=0).astype(BF16)
    w = w_ref[...]
    w_hi = w.astype(BF16)
    w_lo = (w - w_hi.astype(F32)).astype(BF16)
    both = _dot(hi_lo, w_hi)
    o_ref[...] = both[:nb] + (both[nb:] + _dot(c_hi.astype(BF16), w_lo)) + b_ref[...]


def _ada(c, w_ada, b_ada):
    B, D = c.shape
    N = w_ada.shape[1]
    tn = 1024
    return pl.pallas_call(
        _ada_kernel,
        grid=(N // tn,),
        in_specs=[pl.BlockSpec((B, D), lambda j: (0, 0)),
                  pl.BlockSpec((D, tn), lambda j: (0, j)),
                  pl.BlockSpec((1, tn), lambda j: (0, j))],
        out_specs=pl.BlockSpec((B, tn), lambda j: (0, j)),
        out_shape=jax.ShapeDtypeStruct((B, N), F32),
        compiler_params=_cparams("arbitrary"),
        name="ada",
    )(c, w_ada, b_ada.reshape(1, N))


def _pool_mix(u, halo, pos0, w_ref, sc_ref, o_ref, gd):
    rows = u.shape[0]
    tpos = lax.broadcasted_iota(jnp.int32, (rows, 1), 0) + pos0 + 1
    for g in range(POOL_GROUPS):
        win = POOL_WINDOWS[g]
        cols = slice(g * gd, (g + 1) * gd)
        s = jnp.concatenate([halo[:, cols], u[:, cols]], axis=0)
        k = 1
        while k < win:
            s = s + pltpu.roll(s, k, axis=0)
            k *= 2
        cnt = jnp.minimum(tpos, win).astype(F32)
        diff = s[POOL_HALO:, :] / cnt - u[:, cols]
        y = _dot(diff.astype(BF16), w_ref[g].astype(BF16)) * sc_ref[:, cols]
        o_ref[:, cols] = y.astype(o_ref.dtype)


def _inproj_kernel(x_ref, mod_ref, g_ref, w_ref, wba_ref, alog_ref, dtb_ref, pw_ref, psc_ref,
                   proj_ref, gates_ref, ypool_ref, h_scr, u_scr, halo_scr, *, seq, gd):
    i = pl.program_id(0)
    j = pl.program_id(1)
    ni = pl.num_programs(0)
    nj = pl.num_programs(1)
    tm = x_ref.shape[0]
    cur = i % 2

    def prologue():
        x = x_ref[...]
        ms = jnp.mean(x * x, axis=-1, keepdims=True)
        y = x * lax.rsqrt(ms + EPS) * g_ref[...]
        h = (y * (1.0 + mod_ref[0, 1:2, :]) + mod_ref[0, 0:1, :]).astype(BF16)
        ba = _dot(h, wba_ref[...])
        lane = lax.broadcasted_iota(jnp.int32, ba.shape, 1)
        a = ba + dtb_ref[...]
        softplus = jnp.maximum(a, 0.0) + jnp.log1p(jnp.exp(-jnp.abs(a)))
        gc = _chunk_cumsum(-jnp.exp(alog_ref[...]) * softplus, DN_CHUNK)
        gates_ref[...] = jnp.where(lane < DN_HEADS, jax.nn.sigmoid(ba), gc)
        return h

    @pl.when((i == 0) & (j == 0))
    def _():
        h = prologue()
        h_scr[0] = h
        halo_scr[...] = jnp.zeros_like(halo_scr)
        u_scr[...] = _dot(h, w_ref[...])

    @pl.when((i > 0) & (j == 0))
    def _():
        u_scr[...] = _dot(h_scr[cur], w_ref[...])

    @pl.when(j == 1)
    def _():
        proj_ref[...] = _dot(h_scr[cur], w_ref[...])
        pos0 = (i * tm) % seq
        u = u_scr[...]
        halo = jnp.where(pos0 > 0, halo_scr[...], 0.0)
        _pool_mix(u, halo, pos0, pw_ref, psc_ref, ypool_ref, gd)
        halo_scr[...] = u[tm - POOL_HALO:, :]

    @pl.when((j > 1) & (j < nj - 1))
    def _():
        proj_ref[...] = _dot(h_scr[cur], w_ref[...])

    @pl.when((j == nj - 1) & (i + 1 < ni))
    def _():
        proj_ref[...] = _dot(h_scr[cur], w_ref[...])
        h_scr[1 - cur] = prologue()

    @pl.when((j == nj - 1) & (i + 1 == ni))
    def _():
        proj_ref[...] = _dot(h_scr[cur], w_ref[...])


def _inproj(x2d, mod3, norm_g, w_all, n_main, w_ba, alog_lanes, dtb_lanes, pool_w, pool_scale, seq):
    T, D = x2d.shape
    G, gd, _ = pool_w.shape
    tm, tn = 1024, 1024
    ni, nj = T // tm, n_main // tn
    assert tm % DN_CHUNK == 0 and seq % tm == 0 and n_main % tn == 0 and G * gd == tn and nj >= 3

    def ahead(i, j):
        return jnp.minimum(i + (j == nj - 1).astype(jnp.int32), ni - 1)

    const = lambda i, j: (0, 0)
    return pl.pallas_call(
        functools.partial(_inproj_kernel, seq=seq, gd=gd),
        grid=(ni, nj),
        in_specs=[pl.BlockSpec((tm, D), lambda i, j: (ahead(i, j), 0)),
                  pl.BlockSpec((1, 6, D), lambda i, j: (ahead(i, j) * tm // seq, 0, 0)),
                  pl.BlockSpec((1, D), const),
                  pl.BlockSpec((D, tn), lambda i, j: (0, j)),
                  pl.BlockSpec((D, LANES), const),
                  pl.BlockSpec((1, LANES), const),
                  pl.BlockSpec((1, LANES), const),
                  pl.BlockSpec((G, gd, gd), lambda i, j: (0, 0, 0)),
                  pl.BlockSpec((1, tn), const)],
        out_specs=[pl.BlockSpec((tm, tn), lambda i, j: (i, jnp.maximum(j - 1, 0))),
                   pl.BlockSpec((tm, LANES), lambda i, j: (ahead(i, j), 0)),
                   pl.BlockSpec((tm, tn), lambda i, j: (i, 0))],
        out_shape=[jax.ShapeDtypeStruct((T, n_main - tn), F32),
                   jax.ShapeDtypeStruct((T, LANES), F32),
                   jax.ShapeDtypeStruct((T, tn), BF16)],
        scratch_shapes=[pltpu.VMEM((2, tm, D), BF16), pltpu.VMEM((tm, tn), F32),
                        pltpu.VMEM((POOL_HALO, tn), F32)],
        compiler_params=_cparams("arbitrary", "arbitrary"),
        name="inproj",
    )(x2d, mod3, norm_g.reshape(1, D), w_all, w_ba, alog_lanes, dtb_lanes, pool_w,
      pool_scale.reshape(1, tn))


def _bmm(a, b):
    return jnp.einsum('nij,njk->nik', a.astype(BF16), b.astype(BF16),
                      preferred_element_type=F32)


def _bmm_nt(a, b):
    return jnp.einsum('nid,njd->nij', a.astype(BF16), b.astype(BF16),
                      preferred_element_type=F32)


def _dn_prepare(q_raw, k_raw, v_raw, cwq, cwk, cwv, beta_col, gc_col, *, chunk):
    seq, d = q_raw.shape
    n = seq // chunk
    top = lax.broadcasted_iota(jnp.int32, (8, d), 0)

    def shift(a, sh):
        r = pltpu.roll(a, sh, axis=0)
        return jnp.concatenate([jnp.where(top >= sh, r[:8], 0.0), r[8:]], axis=0)

    def conv_silu(x, cw):
        assert CONV_WIDTH == 4
        w0, w1, w2, w3 = (cw[j:j + 1, :] for j in range(CONV_WIDTH))
        x1 = shift(x, 1)
        return _silu(x * w3 + x1 * w2 + shift(x * w1 + x1 * w0, 2))

    def l2n(x):
        return x * lax.rsqrt(jnp.sum(x * x, axis=-1, keepdims=True) + EPS)

    qn = l2n(conv_silu(q_raw, cwq)) * (d ** -0.5)
    kn = l2n(conv_silu(k_raw, cwk))
    v = conv_silu(v_raw, cwv)
    beta = jnp.broadcast_to(beta_col, (seq, d))
    gc = jnp.broadcast_to(gc_col, (seq, d))

    gc3 = gc.reshape(n, chunk, d)
    ii = lax.broadcasted_iota(jnp.int32, (chunk, chunk), 0)
    jj = lax.broadcasted_iota(jnp.int32, (chunk, chunk), 1)
    incl = (ii >= jj)[None]
    strict = (ii > jj)[None]
    diff = gc3 - jnp.swapaxes(gc3, 1, 2)
    decay = jnp.where(incl, jnp.exp(jnp.where(incl, diff, 0.0)), 0.0)

    kb = kn * beta
    kn3 = kn.reshape(n, chunk, d)
    lmat = jnp.where(strict, _bmm_nt(kb.reshape(n, chunk, d), kn3) * decay, 0.0)
    qk = jnp.where(incl, _bmm_nt(qn.reshape(n, chunk, d), kn3) * decay, 0.0)

    def merge_mask(lv):
        same = (ii >> (lv + 1)) == (jj >> (lv + 1))
        return (same & (((ii >> lv) & 1) == 1) & (((jj >> lv) & 1) == 0))[None]

    tinv = (ii == jj).astype(F32)[None] - jnp.where(merge_mask(0), lmat, 0.0)
    for lv in range(1, chunk.bit_length() - 1):
        a21 = jnp.where(merge_mask(lv), lmat, 0.0)
        tinv = tinv - _bmm(tinv, _bmm(a21, tinv))

    egc = jnp.exp(gc)
    rhs = jnp.concatenate([v * beta, kb * egc], axis=-1).reshape(n, chunk, 2 * d)
    uw = _bmm(tinv, rhs)
    glast = gc3[:, chunk - 1:chunk, :]
    kdt = jnp.swapaxes(kn3 * jnp.exp(glast - gc3), 1, 2)
    kuw = _bmm(kdt, uw)
    quw = _bmm(qk, uw)
    qp = (qn * egc).reshape(n, chunk, d) - quw[..., d:]
    return quw[..., :d], qp, kuw[..., :d], kuw[..., d:], jnp.exp(glast)


def _dn_kernel(q_ref, k_ref, v_ref, z_ref, gates_ref, cwq_ref, cwk_ref, cwv_ref, on_ref,
               o_ref, o_scr, *, chunk, heads_per_step):
    d = DN_HEAD_DIM
    seq = q_ref.shape[0]
    n = seq // chunk
    lane = lax.broadcasted_iota(jnp.int32, (seq, LANES), 1)
    gates = gates_ref[...]
    prepared = []
    for i in range(heads_per_step):
        head = pl.program_id(1) * heads_per_step + i
        cols = slice(i * d, (i + 1) * d)
        beta_col = jnp.sum(jnp.where(lane == head, gates, 0.0), axis=-1, keepdims=True)
        gc_col = jnp.sum(jnp.where(lane == head + DN_HEADS, gates, 0.0), axis=-1, keepdims=True)
        prepared.append(_dn_prepare(q_ref[:, cols], k_ref[:, cols], v_ref[:, cols],
                                    cwq_ref[:, cols], cwk_ref[:, cols], cwv_ref[:, cols],
                                    beta_col, gc_col, chunk=chunk))

    states = [jnp.zeros((d, d), F32) for _ in range(heads_per_step)]
    for c in range(n):
        for i, (o0, qp, kub, kuw, eglast) in enumerate(prepared):
            sb = states[i].astype(BF16)
            o_scr[c * chunk:(c + 1) * chunk, i * d:(i + 1) * d] = o0[c] + _dot(qp[c].astype(BF16), sb)
            states[i] = states[i] * eglast[c] + kub[c] - _dot(kuw[c].astype(BF16), sb)

    for i in range(heads_per_step):
        cols = slice(i * d, (i + 1) * d)
        o = o_scr[:, cols]
        o = o * lax.rsqrt(jnp.mean(o * o, axis=-1, keepdims=True) + EPS) * on_ref[...]
        o_ref[:, cols] = (o * _silu(z_ref[:, cols])).astype(o_ref.dtype)


def _deltanet(proj, gates, conv_w, o_norm_g, batch, seq, col0):
    T = proj.shape[0]
    d = DN_HEAD_DIM
    H = DN_HEADS
    hps = 2
    w = hps * d
    cb = col0 // w
    nhb = H // hps

    def colspec(off):
        return pl.BlockSpec((seq, w), lambda b, h: (b, off + h))

    def convspec(off):
        return pl.BlockSpec((CONV_WIDTH, w), lambda b, h: (0, off + h))

    return pl.pallas_call(
        functools.partial(_dn_kernel, chunk=DN_CHUNK, heads_per_step=hps),
        grid=(batch, nhb),
        in_specs=[colspec(cb), colspec(cb + nhb), colspec(cb + 2 * nhb), colspec(cb + 3 * nhb),
                  pl.BlockSpec((seq, LANES), lambda b, h: (b, 0)),
                  convspec(0), convspec(nhb), convspec(2 * nhb),
                  pl.BlockSpec((1, d), lambda b, h: (0, 0))],
        out_specs=pl.BlockSpec((seq, w), lambda b, h: (b, h)),
        out_shape=jax.ShapeDtypeStruct((T, H * d), BF16),
        scratch_shapes=[pltpu.VMEM((seq, w), F32)],
        compiler_params=_cparams("arbitrary", "arbitrary"),
        name="deltanet",
    )(proj, proj, proj, proj, gates, conv_w, conv_w, conv_w, o_norm_g.reshape(1, d))


def _route(logits):
    lane = lax.broadcasted_iota(jnp.int32, logits.shape, 1)
    lanef = lane.astype(F32)
    far = float(LANES)

    def first_max(vals):
        m = jnp.max(vals, axis=-1, keepdims=True)
        idx = jnp.min(jnp.where(vals == m, lanef, far), axis=-1, keepdims=True)
        return m, idx

    gl = jnp.where(lane < N_GROUPS, logits, NEG_BIG)
    gmax, gidx = first_max(gl)
    p_top = 1.0 / jnp.sum(jnp.exp(gl - gmax), axis=-1, keepdims=True)
    lo = N_GROUPS + EXPERTS_PER_GROUP * gidx
    el = jnp.where(lanef >= lo, jnp.where(lanef < lo + EXPERTS_PER_GROUP, logits, NEG_BIG), NEG_BIG)
    m1, i1 = first_max(el)
    m2, i2 = first_max(jnp.where(lanef == i1, NEG_BIG, el))
    t = jnp.exp(m2 - m1)
    w1 = p_top / (1.0 + t)
    w2 = w1 * t
    return jnp.where(lane == 0, i1 - N_GROUPS,
                     jnp.where(lane == 1, i2 - N_GROUPS,
                               jnp.where(lane == 2, w1, jnp.where(lane == 3, w2, 0.0))))


def _slab_store(ref, val):
    rows, width = val.shape
    ns = width // LANES
    for s in range(ns):
        ref[pl.ds(s, rows, stride=ns), :] = val[:, s * LANES:(s + 1) * LANES]


def _slab_load(ref, first_row, rows, ns, pitch):
    return [ref[pl.ds(first_row * pitch + s, rows, stride=pitch), :] for s in range(ns)]


def _gather_pitch(ns):
    return ns + 8 if (ns // 8) % 2 == 0 else ns


def _outproj_kernel(yp_ref, yd_ref, wo_ref, x_ref, mod_ref, g_ref, wr_ref, br_ref,
                    x2_ref, h2_ref, route_ref):
    half = yp_ref.shape[1]
    out = _dot(yp_ref[...], wo_ref[:half, :]) + _dot(yd_ref[...], wo_ref[half:, :])
    x2 = x_ref[...] + mod_ref[0, 2:3, :] * out
    x2_ref[...] = x2
    ms = jnp.mean(x2 * x2, axis=-1, keepdims=True)
    y = x2 * lax.rsqrt(ms + EPS) * g_ref[...]
    h2 = y * (1.0 + mod_ref[0, 4:5, :]) + mod_ref[0, 3:4, :]
    _slab_store(h2_ref, h2)
    hi = h2.astype(BF16)
    lo = (h2 - hi.astype(F32)).astype(BF16)
    both = _dot(hi, wr_ref[...])
    logits = both[:, :LANES] + (_dot(lo, wr_ref[:, :LANES]) + both[:, LANES:]) + br_ref[...]
    route_ref[...] = _route(logits)


def _outproj(y_pool, y_dn, w_out_bf, x2d, mod3, norm_g, w_router, b_router, seq):
    T, D = x2d.shape
    half = y_pool.shape[1]
    tm = 512
    row = lambda i: (i, 0)
    const = lambda i: (0, 0)
    return pl.pallas_call(
        _outproj_kernel,
        grid=(T // tm,),
        in_specs=[pl.BlockSpec((tm, half), row),
                  pl.BlockSpec((tm, half), row),
                  pl.BlockSpec((2 * half, D), const, pipeline_mode=pl.Buffered(1)),
                  pl.BlockSpec((tm, D), row),
                  pl.BlockSpec((1, 6, D), lambda i: (i * tm // seq, 0, 0)),
                  pl.BlockSpec((1, D), const),
                  pl.BlockSpec((D, 2 * LANES), const),
                  pl.BlockSpec((1, LANES), const)],
        out_specs=[pl.BlockSpec((tm, D), row),
                   pl.BlockSpec((tm * (D // LANES), LANES), row),
                   pl.BlockSpec((tm, LANES), row)],
        out_shape=[jax.ShapeDtypeStruct((T, D), F32),
                   jax.ShapeDtypeStruct((T * (D // LANES), LANES), F32),
                   jax.ShapeDtypeStruct((T, LANES), F32)],
        compiler_params=_cparams("arbitrary"),
        name="outproj",
    )(y_pool, y_dn, w_out_bf, x2d, mod3, norm_g.reshape(1, D), w_router, b_router)


GATHER_UNROLL = 8


def _row_gather(idx_ref, nrows, ns, pitch, src_hbm, dst, sem):
    def body(r, carry):
        src_row = pl.multiple_of(idx_ref[0, 0, r] * ns, ns)
        dst_row = pl.multiple_of(r * pitch, 8)
        pltpu.make_async_copy(src_hbm.at[pl.ds(src_row, ns), :], dst.at[pl.ds(dst_row, ns), :],
                              sem).start()
        return carry
    lax.fori_loop(0, nrows, body, 0, unroll=GATHER_UNROLL)


def _row_gather_wait(nrows, ns, src_hbm, dst, sem):
    pltpu.make_async_copy(src_hbm.at[pl.ds(0, nrows * ns), :], dst.at[pl.ds(0, nrows * ns), :],
                          sem).wait()


WEIGHT_DMA_PRIORITY = 1


def _expert_kernel(be_ref, first_ref, wslot_ref, nexte_ref, nact_ref, tok_ref, tokn_ref,
                   h_hbm, wg_hbm, wu_hbm, wd_hbm, ys_ref, xbuf, wg_buf, wu_buf, wd_buf,
                   sem, wsem, *, bm, ns, pitch):
    i = pl.program_id(0)
    nact = nact_ref[0]
    slot = i % 2
    wslot = wslot_ref[i]

    def weight_copies(e, s):
        return (pltpu.make_async_copy(wg_hbm.at[e], wg_buf.at[s], wsem.at[s]),
                pltpu.make_async_copy(wu_hbm.at[e], wu_buf.at[s], wsem.at[s]),
                pltpu.make_async_copy(wd_hbm.at[e], wd_buf.at[s], wsem.at[s]))

    @pl.when(i == 0)
    def _():
        for cp in weight_copies(be_ref[0], 0):
            cp.start(priority=WEIGHT_DMA_PRIORITY)
        _row_gather(tok_ref, bm, ns, pitch, h_hbm, xbuf.at[0], sem.at[0])

    @pl.when((first_ref[i] == 1) & (nexte_ref[i] >= 0))
    def _():
        for cp in weight_copies(nexte_ref[i], 1 - wslot):
            cp.start(priority=WEIGHT_DMA_PRIORITY)

    @pl.when(i + 1 < nact)
    def _():
        _row_gather(tokn_ref, bm, ns, pitch, h_hbm, xbuf.at[1 - slot], sem.at[1 - slot])

    @pl.when(first_ref[i] == 1)
    def _():
        for cp in weight_copies(be_ref[i], wslot):
            cp.wait()

    @pl.when(i < nact)
    def _():
        _row_gather_wait(bm, ns, h_hbm, xbuf.at[slot], sem.at[slot])
        xb = jnp.concatenate(_slab_load(xbuf.at[slot], 0, bm, ns, pitch), axis=-1).astype(BF16)
        gate = _dot(xb, wg_buf[wslot].astype(BF16))
        up = _dot(xb, wu_buf[wslot].astype(BF16))
        hid = (_silu(gate) * up).astype(BF16)
        _slab_store(ys_ref, _dot(hid, wd_buf[wslot].astype(BF16)))

    @pl.when(i >= nact)
    def _():
        ys_ref[...] = jnp.zeros_like(ys_ref)


def _experts(h2_slab, block_e, nact, buf_tok, w_gate, w_up, w_down):
    E, D, De = w_gate.shape
    ns = D // LANES
    pitch = _gather_pitch(ns)
    bm = MOE_BLOCK
    n_pad = buf_tok.shape[0]
    nb = n_pad // bm
    tok3 = buf_tok.reshape(nb, 1, bm)

    idx = jnp.arange(nb, dtype=jnp.int32)
    active = idx < nact[0]
    prev_e = jnp.concatenate([jnp.full((1,), -1, jnp.int32), block_e[:-1]])
    first = (active & (block_e != prev_e)).astype(jnp.int32)
    wslot = ((jnp.cumsum(first) - 1) % 2).astype(jnp.int32)
    later_first = (first[None, :] == 1) & (idx[None, :] > idx[:, None])
    nxt = jnp.min(jnp.where(later_first, idx[None, :], nb), axis=1)
    next_e = jnp.where(nxt < nb, _lookup(block_e, jnp.minimum(nxt, nb - 1)), -1).astype(jnp.int32)

    any_spec = pl.BlockSpec(memory_space=pl.ANY)
    grid_spec = pltpu.PrefetchScalarGridSpec(
        num_scalar_prefetch=5,
        grid=(nb,),
        in_specs=[pl.BlockSpec((1, 1, bm), lambda i, *_: (i, 0, 0), memory_space=pltpu.SMEM),
                  pl.BlockSpec((1, 1, bm), lambda i, *_: (jnp.minimum(i + 1, nb - 1), 0, 0),
                               memory_space=pltpu.SMEM),
                  any_spec, any_spec, any_spec, any_spec],
        out_specs=pl.BlockSpec((bm * ns, LANES), lambda i, *_: (i, 0)),
        scratch_shapes=[pltpu.VMEM((2, bm * pitch, LANES), F32),
                        pltpu.VMEM((2, D, De), F32), pltpu.VMEM((2, D, De), F32),
                        pltpu.VMEM((2, De, D), F32),
                        pltpu.SemaphoreType.DMA((2,)), pltpu.SemaphoreType.DMA((2,))],
    )
    return pl.pallas_call(
        functools.partial(_expert_kernel, bm=bm, ns=ns, pitch=pitch),
        grid_spec=grid_spec,
        out_shape=jax.ShapeDtypeStruct((n_pad * ns, LANES), F32),
        compiler_params=_cparams("arbitrary"),
        name="experts",
    )(block_e, first, wslot, next_e, nact, tok3, tok3, h2_slab, w_gate, w_up, w_down)


def _combine_kernel(pos_ref, posn_ref, ys_hbm, x2_ref, mod_ref, route_ref, g_ref, o_ref,
                    ybuf, sem, *, tm, ns, pitch):
    i = pl.program_id(0)
    nsteps = pl.num_programs(0)
    slot = i % 2
    nrows = TOP_K * tm

    @pl.when(i == 0)
    def _():
        _row_gather(pos_ref, nrows, ns, pitch, ys_hbm, ybuf.at[0], sem.at[0])

    @pl.when(i + 1 < nsteps)
    def _():
        _row_gather(posn_ref, nrows, ns, pitch, ys_hbm, ybuf.at[1 - slot], sem.at[1 - slot])

    _row_gather_wait(nrows, ns, ys_hbm, ybuf.at[slot], sem.at[slot])
    route = route_ref[...]
    w0 = route[:, 2:3]
    w1 = route[:, 3:4]
    y0 = _slab_load(ybuf.at[slot], 0, tm, ns, pitch)
    y1 = _slab_load(ybuf.at[slot], tm, tm, ns, pitch)
    y = jnp.concatenate([w0 * a + w1 * b for a, b in zip(y0, y1)], axis=-1)
    x3 = x2_ref[...] + mod_ref[0, 5:6, :] * y
    ms = jnp.mean(x3 * x3, axis=-1, keepdims=True)
    o_ref[...] = x3 * lax.rsqrt(ms + EPS) * g_ref[...]


def _combine(ys_slab, pos, x2, mod3, route, norm_g, seq):
    T, D = x2.shape
    ns = D // LANES
    pitch = _gather_pitch(ns)
    tm = 256
    nt = T // tm
    pos3 = pos.reshape(nt, tm, TOP_K).transpose(0, 2, 1).reshape(nt, 1, TOP_K * tm)
    row = lambda i: (i, 0)
    return pl.pallas_call(
        functools.partial(_combine_kernel, tm=tm, ns=ns, pitch=pitch),
        grid=(nt,),
        in_specs=[pl.BlockSpec((1, 1, TOP_K * tm), lambda i: (i, 0, 0), memory_space=pltpu.SMEM),
                  pl.BlockSpec((1, 1, TOP_K * tm), lambda i: (jnp.minimum(i + 1, nt - 1), 0, 0),
                               memory_space=pltpu.SMEM),
                  pl.BlockSpec(memory_space=pl.ANY),
                  pl.BlockSpec((tm, D), row),
                  pl.BlockSpec((1, 6, D), lambda i: (i * tm // seq, 0, 0)),
                  pl.BlockSpec((tm, LANES), row),
                  pl.BlockSpec((1, D), lambda i: (0, 0))],
        out_specs=pl.BlockSpec((tm, D), row),
        out_shape=jax.ShapeDtypeStruct((T, D), F32),
        scratch_shapes=[pltpu.VMEM((2, TOP_K * tm * pitch, LANES), F32),
                        pltpu.SemaphoreType.DMA((2,))],
        compiler_params=_cparams("arbitrary"),
        name="combine",
    )(pos3, pos3, ys_slab, x2, mod3, route, norm_g.reshape(1, D))


def _lookup(table, idx):
    hit = idx[..., None] == jnp.arange(table.shape[0], dtype=idx.dtype)
    return jnp.sum(jnp.where(hit, table, 0), axis=-1)


def _dispatch_plan(route, n_tokens):
    A = n_tokens * TOP_K
    bm = MOE_BLOCK
    nb = (A + N_EXPERTS * (bm - 1)) // bm + 1
    flat_e = route[:, :TOP_K].astype(jnp.int32).reshape(A)
    iota = jnp.arange(A, dtype=jnp.int32)
    _, order = lax.sort_key_val(flat_e, iota)
    _, inv = lax.sort_key_val(order, iota)
    onehot = flat_e[:, None] == jnp.arange(N_EXPERTS, dtype=jnp.int32)[None, :]
    counts = lax.optimization_barrier(jnp.sum(onehot, axis=0, dtype=jnp.int32))
    padded = (counts + bm - 1) // bm * bm
    pad_end = jnp.cumsum(padded)
    pad_start = pad_end - padded
    start = jnp.cumsum(counts) - counts
    pos = inv + jnp.sum(jnp.where(onehot, (pad_start - start)[None, :], 0), axis=1)
    block_start = jnp.arange(nb, dtype=jnp.int32) * bm
    block_e = jnp.minimum(jnp.sum(pad_end[None, :] <= block_start[:, None], axis=1),
                          N_EXPERTS - 1).astype(jnp.int32)
    r = (block_start - _lookup(pad_start, block_e))[:, None] + jnp.arange(bm, dtype=jnp.int32)[None, :]
    src = jnp.clip(_lookup(start, block_e)[:, None] + r, 0, A - 1)
    buf_tok = jnp.where(r < _lookup(counts, block_e)[:, None], order[src] // TOP_K, 0)
    nact = (pad_end[-1:] // bm).astype(jnp.int32)
    return buf_tok.reshape(nb * bm), pos.astype(jnp.int32), block_e, nact


def kernel(x, c, w_ada, b_ada, norm1_g, w_in, pool_w, pool_scale, conv_w, a_log, dt_bias,
           o_norm_g, w_out, norm2_g, w_router_group, b_router_group, w_router_expert,
           b_router_expert, w_gate, w_up, w_down, norm_f_g):
    B, S, D = x.shape
    T = B * S
    depth = w_ada.shape[0]
    pool_width = pool_w.shape[1] * pool_w.shape[2]
    n_main = pool_width + 4 * DN_HEADS * DN_HEAD_DIM
    n_route = N_GROUPS + N_EXPERTS

    assert depth == 1, "kernel supports the single-layer configuration only"
    l = 0
    xt = x.reshape(T, D)
    mod3 = _ada(c, w_ada[l], b_ada[l]).reshape(B, 6, D)

    w_all = w_in[l].astype(BF16)
    w_ba = jnp.pad(w_in[l, :, n_main:], ((0, 0), (0, LANES - 2 * DN_HEADS))).astype(BF16)
    gate_pad = (DN_HEADS, LANES - 2 * DN_HEADS)
    alog_lanes = jnp.pad(a_log[l], gate_pad).reshape(1, LANES)
    dtb_lanes = jnp.pad(dt_bias[l], gate_pad).reshape(1, LANES)
    proj, gates, y_pool = _inproj(xt, mod3, norm1_g[l], w_all, n_main, w_ba, alog_lanes, dtb_lanes,
                                  pool_w[l], pool_scale[l], S)
    y_dn = _deltanet(proj, gates, conv_w[l], o_norm_g[l], B, S, 0)

    w_router = jnp.pad(jnp.concatenate([w_router_group[l], w_router_expert[l]], axis=1),
                       ((0, 0), (0, LANES - n_route)))
    b_router = jnp.pad(jnp.concatenate([b_router_group[l], b_router_expert[l]]),
                       (0, LANES - n_route)).reshape(1, LANES)
    w_router_hi = w_router.astype(BF16)
    w_router_lo = (w_router - w_router_hi.astype(F32)).astype(BF16)
    x2, h2, route = _outproj(y_pool, y_dn, w_out[l].astype(BF16), xt, mod3, norm2_g[l],
                             jnp.concatenate([w_router_hi, w_router_lo], axis=1), b_router, S)

    buf_tok, pos, block_e, nact = _dispatch_plan(route, T)
    ys = _experts(h2, block_e, nact, buf_tok, w_gate[l], w_up[l], w_down[l])
    out = _combine(ys, pos, x2, mod3, route, norm_f_g, S)
    return out.reshape(B, S, D)
```

```python
import functools

import jax
import jax.numpy as jnp
from jax import lax
from jax.experimental import pallas as pl
from jax.experimental.pallas import tpu as pltpu

F32 = jnp.float32
BF16 = jnp.bfloat16

POOL_GROUPS = 4
POOL_WINDOWS = (2, 4, 8, 16)
POOL_HALO = 16
DN_HEADS = 8
DN_HEAD_DIM = 128
CONV_WIDTH = 4
DN_CHUNK = 128
N_GROUPS = 4
EXPERTS_PER_GROUP = 8
N_EXPERTS = N_GROUPS * EXPERTS_PER_GROUP
TOP_K = 2
MOE_BLOCK = 256
EPS = 1e-6
LANES = 128
NEG_BIG = -3.0e38
VMEM_LIMIT = 60 * 1024 * 1024


def _silu(x):
    half = 0.5 * x
    return half * (1.0 + jnp.tanh(half))


def _chunk_cumsum(x, chunk):
    rmod = lax.broadcasted_iota(jnp.int32, x.shape, 0) & (chunk - 1)
    k = 1
    while k < chunk:
        x = x + jnp.where(rmod >= k, pltpu.roll(x, k, axis=0), 0.0)
        k *= 2
    return x


def _dot(a, b):
    return jnp.dot(a, b, preferred_element_type=F32)


def _cparams(*sem):
    return pltpu.CompilerParams(dimension_semantics=sem, vmem_limit_bytes=VMEM_LIMIT)


def _ada_kernel(c_ref, w_ref, b_ref, o_ref):
    ca = _silu(c_ref[...])
    nb = ca.shape[0]
    c_hi = ca.astype(BF16).astype(F32)
    hi_lo = jnp.concatenate([c_hi, ca - c_hi], axis=0).astype(BF16)
    w = w_ref[...]
    w_hi = w.astype(BF16)
    w_lo = (w - w_hi.astype(F32)).astype(BF16)
    both = _dot(hi_lo, w_hi)
    o_ref[...] = both[:nb] + (both[nb:] + _dot(c_hi.astype(BF16), w_lo)) + b_ref[...]


def _ada(c, w_ada, b_ada):
    B, D = c.shape
    N = w_ada.shape[1]
    tn = 1024
    return pl.pallas_call(
        _ada_kernel,
        grid=(N // tn,),
        in_specs=[pl.BlockSpec((B, D), lambda j: (0, 0)),
                  pl.BlockSpec((D, tn), lambda j: (0, j)),
                  pl.BlockSpec((1, tn), lambda j: (0, j))],
        out_specs=pl.BlockSpec((B, tn), lambda j: (0, j)),
        out_shape=jax.ShapeDtypeStruct((B, N), F32),
        compiler_params=_cparams("arbitrary"),
        name="ada",
    )(c, w_ada, b_ada.reshape(1, N))


def _pool_mix(u, halo, pos0, w_ref, sc_ref, o_ref, gd):
    rows = u.shape[0]
    tpos = lax.broadcasted_iota(jnp.int32, (rows, 1), 0) + pos0 + 1
    for g in range(POOL_GROUPS):
        win = POOL_WINDOWS[g]
        cols = slice(g * gd, (g + 1) * gd)
        s = jnp.concatenate([halo[:, cols], u[:, cols]], axis=0)
        k = 1
        while k < win:
            s = s + pltpu.roll(s, k, axis=0)
            k *= 2
        cnt = jnp.minimum(tpos, win).astype(F32)
        diff = s[POOL_HALO:, :] / cnt - u[:, cols]
        y = _dot(diff.astype(BF16), w_ref[g].astype(BF16)) * sc_ref[:, cols]
        o_ref[:, cols] = y.astype(o_ref.dtype)


def _inproj_kernel(x_ref, mod_ref, g_ref, w_ref, wba_ref, alog_ref, dtb_ref, pw_ref, psc_ref,
                   proj_ref, gates_ref, ypool_ref, h_scr, u_scr, halo_scr, *, seq, gd):
    i = pl.program_id(0)
    j = pl.program_id(1)
    ni = pl.num_programs(0)
    nj = pl.num_programs(1)
    tm = x_ref.shape[0]
    cur = i % 2

    def prologue():
        x = x_ref[...]
        ms = jnp.mean(x * x, axis=-1, keepdims=True)
        y = x * lax.rsqrt(ms + EPS) * g_ref[...]
        h = (y * (1.0 + mod_ref[0, 1:2, :]) + mod_ref[0, 0:1, :]).astype(BF16)
        ba = _dot(h, wba_ref[...])
        lane = lax.broadcasted_iota(jnp.int32, ba.shape, 1)
        a = ba + dtb_ref[...]
        softplus = jnp.maximum(a, 0.0) + jnp.log1p(jnp.exp(-jnp.abs(a)))
        gc = _chunk_cumsum(-jnp.exp(alog_ref[...]) * softplus, DN_CHUNK)
        gates_ref[...] = jnp.where(lane < DN_HEADS, jax.nn.sigmoid(ba), gc)
        return h

    @pl.when((i == 0) & (j == 0))
    def _():
        h = prologue()
        h_scr[0] = h
        halo_scr[...] = jnp.zeros_like(halo_scr)
        u_scr[...] = _dot(h, w_ref[...])

    @pl.when((i > 0) & (j == 0))
    def _():
        u_scr[...] = _dot(h_scr[cur], w_ref[...])

    @pl.when(j == 1)
    def _():
        proj_ref[...] = _dot(h_scr[cur], w_ref[...])
        pos0 = (i * tm) % seq
        u = u_scr[...]
        halo = jnp.where(pos0 > 0, halo_scr[...], 0.0)
        _pool_mix(u, halo, pos0, pw_ref, psc_ref, ypool_ref, gd)
        halo_scr[...] = u[tm - POOL_HALO:, :]

    @pl.when((j > 1) & (j < nj - 1))
    def _():
        proj_ref[...] = _dot(h_scr[cur], w_ref[...])

    @pl.when((j == nj - 1) & (i + 1 < ni))
    def _():
        proj_ref[...] = _dot(h_scr[cur], w_ref[...])
        h_scr[1 - cur] = prologue()

    @pl.when((j == nj - 1) & (i + 1 == ni))
    def _():
        proj_ref[...] = _dot(h_scr[cur], w_ref[...])


def _inproj(x2d, mod3, norm_g, w_all, n_main, w_ba, alog_lanes, dtb_lanes, pool_w, pool_scale, seq):
    T, D = x2d.shape
    G, gd, _ = pool_w.shape
    tm, tn = 1024, 1024
    ni, nj = T // tm, n_main // tn
    assert tm % DN_CHUNK == 0 and seq % tm == 0 and n_main % tn == 0 and G * gd == tn and nj >= 3

    def ahead(i, j):
        return jnp.minimum(i + (j == nj - 1).astype(jnp.int32), ni - 1)

    const = lambda i, j: (0, 0)
    return pl.pallas_call(
        functools.partial(_inproj_kernel, seq=seq, gd=gd),
        grid=(ni, nj),
        in_specs=[pl.BlockSpec((tm, D), lambda i, j: (ahead(i, j), 0)),
                  pl.BlockSpec((1, 6, D), lambda i, j: (ahead(i, j) * tm // seq, 0, 0)),
                  pl.BlockSpec((1, D), const),
                  pl.BlockSpec((D, tn), lambda i, j: (0, j)),
                  pl.BlockSpec((D, LANES), const),
                  pl.BlockSpec((1, LANES), const),
                  pl.BlockSpec((1, LANES), const),
                  pl.BlockSpec((G, gd, gd), lambda i, j: (0, 0, 0)),
                  pl.BlockSpec((1, tn), const)],
        out_specs=[pl.BlockSpec((tm, tn), lambda i, j: (i, jnp.maximum(j - 1, 0))),
                   pl.BlockSpec((tm, LANES), lambda i, j: (ahead(i, j), 0)),
                   pl.BlockSpec((tm, tn), lambda i, j: (i, 0))],
        out_shape=[jax.ShapeDtypeStruct((T, n_main - tn), F32),
                   jax.ShapeDtypeStruct((T, LANES), F32),
                   jax.ShapeDtypeStruct((T, tn), BF16)],
        scratch_shapes=[pltpu.VMEM((2, tm, D), BF16), pltpu.VMEM((tm, tn), F32),
                        pltpu.VMEM((POOL_HALO, tn), F32)],
        compiler_params=_cparams("arbitrary", "arbitrary"),
        name="inproj",
    )(x2d, mod3, norm_g.reshape(1, D), w_all, w_ba, alog_lanes, dtb_lanes, pool_w,
      pool_scale.reshape(1, tn))


def _bmm(a, b):
    return jnp.einsum('nij,njk->nik', a.astype(BF16), b.astype(BF16),
                      preferred_element_type=F32)


def _bmm_nt(a, b):
    return jnp.einsum('nid,njd->nij', a.astype(BF16), b.astype(BF16),
                      preferred_element_type=F32)


def _dn_prepare(q_raw, k_raw, v_raw, cwq, cwk, cwv, beta_col, gc_col, *, chunk):
    seq, d = q_raw.shape
    n = seq // chunk
    top = lax.broadcasted_iota(jnp.int32, (8, d), 0)

    def shift(a, sh):
        r = pltpu.roll(a, sh, axis=0)
        return jnp.concatenate([jnp.where(top >= sh, r[:8], 0.0), r[8:]], axis=0)

    def conv_silu(x, cw):
        assert CONV_WIDTH == 4
        w0, w1, w2, w3 = (cw[j:j + 1, :] for j in range(CONV_WIDTH))
        x1 = shift(x, 1)
        return _silu(x * w3 + x1 * w2 + shift(x * w1 + x1 * w0, 2))

    def l2n(x):
        return x * lax.rsqrt(jnp.sum(x * x, axis=-1, keepdims=True) + EPS)

    qn = l2n(conv_silu(q_raw, cwq)) * (d ** -0.5)
    kn = l2n(conv_silu(k_raw, cwk))
    v = conv_silu(v_raw, cwv)
    beta = jnp.broadcast_to(beta_col, (seq, d))
    gc = jnp.broadcast_to(gc_col, (seq, d))

    gc3 = gc.reshape(n, chunk, d)
    ii = lax.broadcasted_iota(jnp.int32, (chunk, chunk), 0)
    jj = lax.broadcasted_iota(jnp.int32, (chunk, chunk), 1)
    incl = (ii >= jj)[None]
    strict = (ii > jj)[None]
    diff = gc3 - jnp.swapaxes(gc3, 1, 2)
    decay = jnp.where(incl, jnp.exp(jnp.where(incl, diff, 0.0)), 0.0)

    kb = kn * beta
    kn3 = kn.reshape(n, chunk, d)
    lmat = jnp.where(strict, _bmm_nt(kb.reshape(n, chunk, d), kn3) * decay, 0.0)
    qk = jnp.where(incl, _bmm_nt(qn.reshape(n, chunk, d), kn3) * decay, 0.0)

    def merge_mask(lv):
        same = (ii >> (lv + 1)) == (jj >> (lv + 1))
        return (same & (((ii >> lv) & 1) == 1) & (((jj >> lv) & 1) == 0))[None]

    tinv = (ii == jj).astype(F32)[None] - jnp.where(merge_mask(0), lmat, 0.0)
    for lv in range(1, chunk.bit_length() - 1):
        a21 = jnp.where(merge_mask(lv), lmat, 0.0)
        tinv = tinv - _bmm(tinv, _bmm(a21, tinv))

    egc = jnp.exp(gc)
    rhs = jnp.concatenate([v * beta, kb * egc], axis=-1).reshape(n, chunk, 2 * d)
    uw = _bmm(tinv, rhs)
    glast = gc3[:, chunk - 1:chunk, :]
    kdt = jnp.swapaxes(kn3 * jnp.exp(glast - gc3), 1, 2)
    kuw = _bmm(kdt, uw)
    quw = _bmm(qk, uw)
    qp = (qn * egc).reshape(n, chunk, d) - quw[..., d:]
    return quw[..., :d], qp, kuw[..., :d], kuw[..., d:], jnp.exp(glast)


def _dn_kernel(q_ref, k_ref, v_ref, z_ref, gates_ref, cwq_ref, cwk_ref, cwv_ref, on_ref,
               o_ref, o_scr, *, chunk, heads_per_step):
    d = DN_HEAD_DIM
    seq = q_ref.shape[0]
    n = seq // chunk
    lane = lax.broadcasted_iota(jnp.int32, (seq, LANES), 1)
    gates = gates_ref[...]
    prepared = []
    for i in range(heads_per_step):
        head = pl.program_id(1) * heads_per_step + i
        cols = slice(i * d, (i + 1) * d)
        beta_col = jnp.sum(jnp.where(lane == head, gates, 0.0), axis=-1, keepdims=True)
        gc_col = jnp.sum(jnp.where(lane == head + DN_HEADS, gates, 0.0), axis=-1, keepdims=True)
        prepared.append(_dn_prepare(q_ref[:, cols], k_ref[:, cols], v_ref[:, cols],
                                    cwq_ref[:, cols], cwk_ref[:, cols], cwv_ref[:, cols],
                                    beta_col, gc_col, chunk=chunk))

    states = [jnp.zeros((d, d), F32) for _ in range(heads_per_step)]
    for c in range(n):
        for i, (o0, qp, kub, kuw, eglast) in enumerate(prepared):
            sb = states[i].astype(BF16)
            o_scr[c * chunk:(c + 1) * chunk, i * d:(i + 1) * d] = o0[c] + _dot(qp[c].astype(BF16), sb)
            states[i] = states[i] * eglast[c] + kub[c] - _dot(kuw[c].astype(BF16), sb)

    for i in range(heads_per_step):
        cols = slice(i * d, (i + 1) * d)
        o = o_scr[:, cols]
        o = o * lax.rsqrt(jnp.mean(o * o, axis=-1, keepdims=True) + EPS) * on_ref[...]
        o_ref[:, cols] = (o * _silu(z_ref[:, cols])).astype(o_ref.dtype)


def _deltanet(proj, gates, conv_w, o_norm_g, batch, seq, col0):
    T = proj.shape[0]
    d = DN_HEAD_DIM
    H = DN_HEADS
    hps = 2
    w = hps * d
    cb = col0 // w
    nhb = H // hps

    def colspec(off):
        return pl.BlockSpec((seq, w), lambda b, h: (b, off + h))

    def convspec(off):
        return pl.BlockSpec((CONV_WIDTH, w), lambda b, h: (0, off + h))

    return pl.pallas_call(
        functools.partial(_dn_kernel, chunk=DN_CHUNK, heads_per_step=hps),
        grid=(batch, nhb),
        in_specs=[colspec(cb), colspec(cb + nhb), colspec(cb + 2 * nhb), colspec(cb + 3 * nhb),
                  pl.BlockSpec((seq, LANES), lambda b, h: (b, 0)),
                  convspec(0), convspec(nhb), convspec(2 * nhb),
                  pl.BlockSpec((1, d), lambda b, h: (0, 0))],
        out_specs=pl.BlockSpec((seq, w), lambda b, h: (b, h)),
        out_shape=jax.ShapeDtypeStruct((T, H * d), BF16),
        scratch_shapes=[pltpu.VMEM((seq, w), F32)],
        compiler_params=_cparams("arbitrary", "arbitrary"),
        name="deltanet",
    )(proj, proj, proj, proj, gates, conv_w, conv_w, conv_w, o_norm_g.reshape(1, d))


def _route(logits):
    lane = lax.broadcasted_iota(jnp.int32, logits.shape, 1)
    lanef = lane.astype(F32)
    far = float(LANES)

    def first_max(vals):
        m = jnp.max(vals, axis=-1, keepdims=True)
        idx = jnp.min(jnp.where(vals == m, lanef, far), axis=-1, keepdims=True)
        return m, idx

    gl = jnp.where(lane < N_GROUPS, logits, NEG_BIG)
    gmax, gidx = first_max(gl)
    p_top = 1.0 / jnp.sum(jnp.exp(gl - gmax), axis=-1, keepdims=True)
    lo = N_GROUPS + EXPERTS_PER_GROUP * gidx
    el = jnp.where(lanef >= lo, jnp.where(lanef < lo + EXPERTS_PER_GROUP, logits, NEG_BIG), NEG_BIG)
    m1, i1 = first_max(el)
    m2, i2 = first_max(jnp.where(lanef == i1, NEG_BIG, el))
    t = jnp.exp(m2 - m1)
    w1 = p_top / (1.0 + t)
    w2 = w1 * t
    return jnp.where(lane == 0, i1 - N_GROUPS,
                     jnp.where(lane == 1, i2 - N_GROUPS,
                               jnp.where(lane == 2, w1, jnp.where(lane == 3, w2, 0.0))))


def _slab_store(ref, val, pitch):
    rows, width = val.shape
    ns = width // LANES
    for s in range(ns):
        ref[pl.ds(s, rows, stride=pitch), :] = val[:, s * LANES:(s + 1) * LANES]
    for s in range(ns, pitch):
        ref[pl.ds(s, rows, stride=pitch), :] = jnp.zeros((rows, LANES), val.dtype)


def _slab_load(ref, first_row, rows, ns, pitch):
    return [ref[pl.ds(first_row * pitch + s, rows, stride=pitch), :] for s in range(ns)]


def _gather_pitch(ns):
    return ns + 8 if (ns // 8) % 2 == 0 else ns


def _outproj_kernel(yp_ref, yd_ref, wo_ref, x_ref, mod_ref, g_ref, wr_ref, br_ref,
                    x2_ref, h2_ref, route_ref):
    half = yp_ref.shape[1]
    out = _dot(yp_ref[...], wo_ref[:half, :]) + _dot(yd_ref[...], wo_ref[half:, :])
    x2 = x_ref[...] + mod_ref[0, 2:3, :] * out
    x2_ref[...] = x2
    ms = jnp.mean(x2 * x2, axis=-1, keepdims=True)
    y = x2 * lax.rsqrt(ms + EPS) * g_ref[...]
    h2 = y * (1.0 + mod_ref[0, 4:5, :]) + mod_ref[0, 3:4, :]
    _slab_store(h2_ref, h2, h2_ref.shape[0] // h2.shape[0])
    hi = h2.astype(BF16)
    lo = (h2 - hi.astype(F32)).astype(BF16)
    both = _dot(hi, wr_ref[...])
    logits = both[:, :LANES] + (_dot(lo, wr_ref[:, :LANES]) + both[:, LANES:]) + br_ref[...]
    route_ref[...] = _route(logits)


def _outproj(y_pool, y_dn, w_out_bf, x2d, mod3, norm_g, w_router, b_router, seq):
    T, D = x2d.shape
    half = y_pool.shape[1]
    pitch = _gather_pitch(D // LANES)
    tm = 512
    row = lambda i: (i, 0)
    const = lambda i: (0, 0)
    return pl.pallas_call(
        _outproj_kernel,
        grid=(T // tm,),
        in_specs=[pl.BlockSpec((tm, half), row),
                  pl.BlockSpec((tm, half), row),
                  pl.BlockSpec((2 * half, D), const, pipeline_mode=pl.Buffered(1)),
                  pl.BlockSpec((tm, D), row),
                  pl.BlockSpec((1, 6, D), lambda i: (i * tm // seq, 0, 0)),
                  pl.BlockSpec((1, D), const),
                  pl.BlockSpec((D, 2 * LANES), const),
                  pl.BlockSpec((1, LANES), const)],
        out_specs=[pl.BlockSpec((tm, D), row),
                   pl.BlockSpec((tm * pitch, LANES), row),
                   pl.BlockSpec((tm, LANES), row)],
        out_shape=[jax.ShapeDtypeStruct((T, D), F32),
                   jax.ShapeDtypeStruct((T * pitch, LANES), F32),
                   jax.ShapeDtypeStruct((T, LANES), F32)],
        compiler_params=_cparams("arbitrary"),
        name="outproj",
    )(y_pool, y_dn, w_out_bf, x2d, mod3, norm_g.reshape(1, D), w_router, b_router)


GATHER_UNROLL = 16


def _row_gather(idx_ref, nrows, ns, pitch, src_hbm, dst, sem):
    def body(r, carry):
        src_row = pl.multiple_of(idx_ref[0, 0, r], 8)
        dst_row = pl.multiple_of(r * pitch, 8)
        pltpu.make_async_copy(src_hbm.at[pl.ds(src_row, ns), :], dst.at[pl.ds(dst_row, ns), :],
                              sem).start()
        return carry
    lax.fori_loop(0, nrows, body, 0, unroll=GATHER_UNROLL)


def _row_gather_wait(nrows, ns, src_hbm, dst, sem):
    pltpu.make_async_copy(src_hbm.at[pl.ds(0, nrows * ns), :], dst.at[pl.ds(0, nrows * ns), :],
                          sem).wait()


WEIGHT_DMA_PRIORITY = 1


def _expert_kernel(be_ref, first_ref, wslot_ref, nexte_ref, nact_ref, tok_ref, tokn_ref,
                   h_hbm, wg_hbm, wu_hbm, wd_hbm, ys_ref, xbuf, wg_buf, wu_buf, wd_buf,
                   sem, wsem, *, bm, ns, pitch):
    i = pl.program_id(0)
    nact = nact_ref[0]
    slot = i % 2
    wslot = wslot_ref[i]

    def weight_copies(e, s):
        return (pltpu.make_async_copy(wg_hbm.at[e], wg_buf.at[s], wsem.at[s]),
                pltpu.make_async_copy(wu_hbm.at[e], wu_buf.at[s], wsem.at[s]),
                pltpu.make_async_copy(wd_hbm.at[e], wd_buf.at[s], wsem.at[s]))

    @pl.when(i == 0)
    def _():
        for cp in weight_copies(be_ref[0], 0):
            cp.start(priority=WEIGHT_DMA_PRIORITY)
        _row_gather(tok_ref, bm, ns, pitch, h_hbm, xbuf.at[0], sem.at[0])

    @pl.when((first_ref[i] == 1) & (nexte_ref[i] >= 0))
    def _():
        for cp in weight_copies(nexte_ref[i], 1 - wslot):
            cp.start(priority=WEIGHT_DMA_PRIORITY)

    @pl.when(i + 1 < nact)
    def _():
        _row_gather(tokn_ref, bm, ns, pitch, h_hbm, xbuf.at[1 - slot], sem.at[1 - slot])

    @pl.when(first_ref[i] == 1)
    def _():
        for cp in weight_copies(be_ref[i], wslot):
            cp.wait()

    @pl.when(i < nact)
    def _():
        _row_gather_wait(bm, ns, h_hbm, xbuf.at[slot], sem.at[slot])
        xb = jnp.concatenate(_slab_load(xbuf.at[slot], 0, bm, ns, pitch), axis=-1).astype(BF16)
        gate = _dot(xb, wg_buf[wslot].astype(BF16))
        up = _dot(xb, wu_buf[wslot].astype(BF16))
        hid = (_silu(gate) * up).astype(BF16)
        _slab_store(ys_ref, _dot(hid, wd_buf[wslot].astype(BF16)), pitch)

    @pl.when(i >= nact)
    def _():
        ys_ref[...] = jnp.zeros_like(ys_ref)


def _experts(h2_slab, block_e, nact, buf_tok, w_gate, w_up, w_down):
    E, D, De = w_gate.shape
    ns = D // LANES
    pitch = _gather_pitch(ns)
    bm = MOE_BLOCK
    n_pad = buf_tok.shape[0]
    nb = n_pad // bm
    tok3 = (buf_tok * pitch).reshape(nb, 1, bm)

    idx = jnp.arange(nb, dtype=jnp.int32)
    active = idx < nact[0]
    prev_e = jnp.concatenate([jnp.full((1,), -1, jnp.int32), block_e[:-1]])
    first = (active & (block_e != prev_e)).astype(jnp.int32)
    wslot = ((jnp.cumsum(first) - 1) % 2).astype(jnp.int32)
    later_first = (first[None, :] == 1) & (idx[None, :] > idx[:, None])
    nxt = jnp.min(jnp.where(later_first, idx[None, :], nb), axis=1)
    next_e = jnp.where(nxt < nb, _lookup(block_e, jnp.minimum(nxt, nb - 1)), -1).astype(jnp.int32)

    any_spec = pl.BlockSpec(memory_space=pl.ANY)
    grid_spec = pltpu.PrefetchScalarGridSpec(
        num_scalar_prefetch=5,
        grid=(nb,),
        in_specs=[pl.BlockSpec((1, 1, bm), lambda i, *_: (i, 0, 0), memory_space=pltpu.SMEM),
                  pl.BlockSpec((1, 1, bm), lambda i, *_: (jnp.minimum(i + 1, nb - 1), 0, 0),
                               memory_space=pltpu.SMEM),
                  any_spec, any_spec, any_spec, any_spec],
        out_specs=pl.BlockSpec((bm * pitch, LANES), lambda i, *_: (i, 0)),
        scratch_shapes=[pltpu.VMEM((2, bm * pitch, LANES), F32),
                        pltpu.VMEM((2, D, De), F32), pltpu.VMEM((2, D, De), F32),
                        pltpu.VMEM((2, De, D), F32),
                        pltpu.SemaphoreType.DMA((2,)), pltpu.SemaphoreType.DMA((2,))],
    )
    return pl.pallas_call(
        functools.partial(_expert_kernel, bm=bm, ns=ns, pitch=pitch),
        grid_spec=grid_spec,
        out_shape=jax.ShapeDtypeStruct((n_pad * pitch, LANES), F32),
        compiler_params=_cparams("arbitrary"),
        name="experts",
    )(block_e, first, wslot, next_e, nact, tok3, tok3, h2_slab, w_gate, w_up, w_down)


def _combine_kernel(pos_ref, posn_ref, ys_hbm, x2_ref, mod_ref, route_ref, g_ref, o_ref,
                    ybuf, sem, *, tm, ns, pitch):
    i = pl.program_id(0)
    nsteps = pl.num_programs(0)
    slot = i % 2
    nrows = TOP_K * tm

    @pl.when(i == 0)
    def _():
        _row_gather(pos_ref, nrows, ns, pitch, ys_hbm, ybuf.at[0], sem.at[0])

    @pl.when(i + 1 < nsteps)
    def _():
        _row_gather(posn_ref, nrows, ns, pitch, ys_hbm, ybuf.at[1 - slot], sem.at[1 - slot])

    _row_gather_wait(nrows, ns, ys_hbm, ybuf.at[slot], sem.at[slot])
    route = route_ref[...]
    w0 = route[:, 2:3]
    w1 = route[:, 3:4]
    y0 = _slab_load(ybuf.at[slot], 0, tm, ns, pitch)
    y1 = _slab_load(ybuf.at[slot], tm, tm, ns, pitch)
    y = jnp.concatenate([w0 * a + w1 * b for a, b in zip(y0, y1)], axis=-1)
    x3 = x2_ref[...] + mod_ref[0, 5:6, :] * y
    ms = jnp.mean(x3 * x3, axis=-1, keepdims=True)
    o_ref[...] = x3 * lax.rsqrt(ms + EPS) * g_ref[...]


def _combine(ys_slab, pos, x2, mod3, route, norm_g, seq):
    T, D = x2.shape
    ns = D // LANES
    pitch = _gather_pitch(ns)
    tm = 256
    nt = T // tm
    pos3 = (pos * pitch).reshape(nt, tm, TOP_K).transpose(0, 2, 1).reshape(nt, 1, TOP_K * tm)
    row = lambda i: (i, 0)
    return pl.pallas_call(
        functools.partial(_combine_kernel, tm=tm, ns=ns, pitch=pitch),
        grid=(nt,),
        in_specs=[pl.BlockSpec((1, 1, TOP_K * tm), lambda i: (i, 0, 0), memory_space=pltpu.SMEM),
                  pl.BlockSpec((1, 1, TOP_K * tm), lambda i: (jnp.minimum(i + 1, nt - 1), 0, 0),
                               memory_space=pltpu.SMEM),
                  pl.BlockSpec(memory_space=pl.ANY),
                  pl.BlockSpec((tm, D), row),
                  pl.BlockSpec((1, 6, D), lambda i: (i * tm // seq, 0, 0)),
                  pl.BlockSpec((tm, LANES), row),
                  pl.BlockSpec((1, D), lambda i: (0, 0))],
        out_specs=pl.BlockSpec((tm, D), row),
        out_shape=jax.ShapeDtypeStruct((T, D), F32),
        scratch_shapes=[pltpu.VMEM((2, TOP_K * tm * pitch, LANES), F32),
                        pltpu.SemaphoreType.DMA((2,))],
        compiler_params=_cparams("arbitrary"),
        name="combine",
    )(pos3, pos3, ys_slab, x2, mod3, route, norm_g.reshape(1, D))


def _lookup(table, idx):
    hit = idx[..., None] == jnp.arange(table.shape[0], dtype=idx.dtype)
    return jnp.sum(jnp.where(hit, table, 0), axis=-1)


def _dispatch_plan(route, n_tokens):
    A = n_tokens * TOP_K
    bm = MOE_BLOCK
    nb = (A + N_EXPERTS * (bm - 1)) // bm + 1
    flat_e = route[:, :TOP_K].astype(jnp.int32).reshape(A)
    iota = jnp.arange(A, dtype=jnp.int32)
    _, order = lax.sort_key_val(flat_e, iota)
    _, inv = lax.sort_key_val(order, iota)
    onehot = flat_e[:, None] == jnp.arange(N_EXPERTS, dtype=jnp.int32)[None, :]
    counts = lax.optimization_barrier(jnp.sum(onehot, axis=0, dtype=jnp.int32))
    padded = (counts + bm - 1) // bm * bm
    pad_end = jnp.cumsum(padded)
    pad_start = pad_end - padded
    start = jnp.cumsum(counts) - counts
    pos = inv + jnp.sum(jnp.where(onehot, (pad_start - start)[None, :], 0), axis=1)
    block_start = jnp.arange(nb, dtype=jnp.int32) * bm
    block_e = jnp.minimum(jnp.sum(pad_end[None, :] <= block_start[:, None], axis=1),
                          N_EXPERTS - 1).astype(jnp.int32)
    r = (block_start - _lookup(pad_start, block_e))[:, None] + jnp.arange(bm, dtype=jnp.int32)[None, :]
    src = jnp.clip(_lookup(start, block_e)[:, None] + r, 0, A - 1)
    buf_tok = jnp.where(r < _lookup(counts, block_e)[:, None], order[src] // TOP_K, 0)
    nact = (pad_end[-1:] // bm).astype(jnp.int32)
    return buf_tok.reshape(nb * bm), pos.astype(jnp.int32), block_e, nact


def kernel(x, c, w_ada, b_ada, norm1_g, w_in, pool_w, pool_scale, conv_w, a_log, dt_bias,
           o_norm_g, w_out, norm2_g, w_router_group, b_router_group, w_router_expert,
           b_router_expert, w_gate, w_up, w_down, norm_f_g):
    B, S, D = x.shape
    T = B * S
    depth = w_ada.shape[0]
    pool_width = pool_w.shape[1] * pool_w.shape[2]
    n_main = pool_width + 4 * DN_HEADS * DN_HEAD_DIM
    n_route = N_GROUPS + N_EXPERTS

    assert depth == 1, "kernel supports the single-layer configuration only"
    l = 0
    xt = x.reshape(T, D)
    mod3 = _ada(c, w_ada[l], b_ada[l]).reshape(B, 6, D)

    w_all = w_in[l].astype(BF16)
    w_ba = jnp.pad(w_in[l, :, n_main:], ((0, 0), (0, LANES - 2 * DN_HEADS))).astype(BF16)
    gate_pad = (DN_HEADS, LANES - 2 * DN_HEADS)
    alog_lanes = jnp.pad(a_log[l], gate_pad).reshape(1, LANES)
    dtb_lanes = jnp.pad(dt_bias[l], gate_pad).reshape(1, LANES)
    proj, gates, y_pool = _inproj(xt, mod3, norm1_g[l], w_all, n_main, w_ba, alog_lanes, dtb_lanes,
                                  pool_w[l], pool_scale[l], S)
    y_dn = _deltanet(proj, gates, conv_w[l], o_norm_g[l], B, S, 0)

    w_router = jnp.pad(jnp.concatenate([w_router_group[l], w_router_expert[l]], axis=1),
                       ((0, 0), (0, LANES - n_route)))
    b_router = jnp.pad(jnp.concatenate([b_router_group[l], b_router_expert[l]]),
                       (0, LANES - n_route)).reshape(1, LANES)
    w_router_hi = w_router.astype(BF16)
    w_router_lo = (w_router - w_router_hi.astype(F32)).astype(BF16)
    x2, h2, route = _outproj(y_pool, y_dn, w_out[l].astype(BF16), xt, mod3, norm2_g[l],
                             jnp.concatenate([w_router_hi, w_router_lo], axis=1), b_router, S)

    buf_tok, pos, block_e, nact = _dispatch_plan(route, T)
    ys = _experts(h2, block_e, nact, buf_tok, w_gate[l], w_up[l], w_down[l])
    out = _combine(ys, pos, x2, mod3, route, norm_f_g, S)
    return out.reshape(B, S, D)
```

```python
import functools

import jax
import jax.numpy as jnp
from jax import lax
from jax.experimental import pallas as pl
from jax.experimental.pallas import tpu as pltpu

F32 = jnp.float32
BF16 = jnp.bfloat16

POOL_GROUPS = 4
POOL_WINDOWS = (2, 4, 8, 16)
POOL_HALO = 16
DN_HEADS = 8
DN_HEAD_DIM = 128
CONV_WIDTH = 4
DN_CHUNK = 128
N_GROUPS = 4
EXPERTS_PER_GROUP = 8
N_EXPERTS = N_GROUPS * EXPERTS_PER_GROUP
TOP_K = 2
MOE_BLOCK = 256
EPS = 1e-6
LANES = 128
NEG_BIG = -3.0e38
VMEM_LIMIT = 60 * 1024 * 1024


def _silu(x):
    half = 0.5 * x
    return half * (1.0 + jnp.tanh(half))


def _chunk_cumsum(x, chunk):
    rmod = lax.broadcasted_iota(jnp.int32, x.shape, 0) & (chunk - 1)
    k = 1
    while k < chunk:
        x = x + jnp.where(rmod >= k, pltpu.roll(x, k, axis=0), 0.0)
        k *= 2
    return x


def _dot(a, b):
    return jnp.dot(a, b, preferred_element_type=F32)


def _cparams(*sem):
    return pltpu.CompilerParams(dimension_semantics=sem, vmem_limit_bytes=VMEM_LIMIT)


def _ada_kernel(c_ref, w_ref, b_ref, o_ref):
    ca = _silu(c_ref[...])
    nb = ca.shape[0]
    c_hi = ca.astype(BF16).astype(F32)
    hi_lo = jnp.concatenate([c_hi, ca - c_hi], axis=0).astype(BF16)
    w = w_ref[...]
    w_hi = w.astype(BF16)
    w_lo = (w - w_hi.astype(F32)).astype(BF16)
    both = _dot(hi_lo, w_hi)
    o_ref[...] = both[:nb] + (both[nb:] + _dot(c_hi.astype(BF16), w_lo)) + b_ref[...]


def _ada(c, w_ada, b_ada):
    B, D = c.shape
    N = w_ada.shape[1]
    tn = 1024
    return pl.pallas_call(
        _ada_kernel,
        grid=(N // tn,),
        in_specs=[pl.BlockSpec((B, D), lambda j: (0, 0)),
                  pl.BlockSpec((D, tn), lambda j: (0, j)),
                  pl.BlockSpec((1, tn), lambda j: (0, j))],
        out_specs=pl.BlockSpec((B, tn), lambda j: (0, j)),
        out_shape=jax.ShapeDtypeStruct((B, N), F32),
        compiler_params=_cparams("arbitrary"),
        name="ada",
    )(c, w_ada, b_ada.reshape(1, N))


def _pool_mix(u, halo, pos0, w_ref, sc_ref, o_ref, gd):
    rows = u.shape[0]
    tpos = lax.broadcasted_iota(jnp.int32, (rows, 1), 0) + pos0 + 1
    for g in range(POOL_GROUPS):
        win = POOL_WINDOWS[g]
        cols = slice(g * gd, (g + 1) * gd)
        s = jnp.concatenate([halo[:, cols], u[:, cols]], axis=0)
        k = 1
        while k < win:
            s = s + pltpu.roll(s, k, axis=0)
            k *= 2
        cnt = jnp.minimum(tpos, win).astype(F32)
        diff = s[POOL_HALO:, :] / cnt - u[:, cols]
        y = _dot(diff.astype(BF16), w_ref[g].astype(BF16)) * sc_ref[:, cols]
        o_ref[:, cols] = y.astype(o_ref.dtype)


def _inproj_kernel(x_ref, mod_ref, g_ref, w_ref, wba_ref, alog_ref, dtb_ref, pw_ref, psc_ref,
                   proj_ref, gates_ref, ypool_ref, h_scr, u_scr, halo_scr, *, seq, gd):
    i = pl.program_id(0)
    j = pl.program_id(1)
    ni = pl.num_programs(0)
    nj = pl.num_programs(1)
    tm = x_ref.shape[0]
    cur = i % 2

    def prologue():
        x = x_ref[...]
        ms = jnp.mean(x * x, axis=-1, keepdims=True)
        y = x * lax.rsqrt(ms + EPS) * g_ref[...]
        h = (y * (1.0 + mod_ref[0, 1:2, :]) + mod_ref[0, 0:1, :]).astype(BF16)
        ba = _dot(h, wba_ref[...])
        lane = lax.broadcasted_iota(jnp.int32, ba.shape, 1)
        a = ba + dtb_ref[...]
        softplus = jnp.maximum(a, 0.0) + jnp.log1p(jnp.exp(-jnp.abs(a)))
        gc = _chunk_cumsum(-jnp.exp(alog_ref[...]) * softplus, DN_CHUNK)
        gates_ref[...] = jnp.where(lane < DN_HEADS, jax.nn.sigmoid(ba), gc)
        return h

    @pl.when((i == 0) & (j == 0))
    def _():
        h = prologue()
        h_scr[0] = h
        halo_scr[...] = jnp.zeros_like(halo_scr)
        u_scr[...] = _dot(h, w_ref[...])

    @pl.when((i > 0) & (j == 0))
    def _():
        u_scr[...] = _dot(h_scr[cur], w_ref[...])

    @pl.when(j == 1)
    def _():
        proj_ref[...] = _dot(h_scr[cur], w_ref[...])
        pos0 = (i * tm) % seq
        u = u_scr[...]
        halo = jnp.where(pos0 > 0, halo_scr[...], 0.0)
        _pool_mix(u, halo, pos0, pw_ref, psc_ref, ypool_ref, gd)
        halo_scr[...] = u[tm - POOL_HALO:, :]

    @pl.when((j > 1) & (j < nj - 1))
    def _():
        proj_ref[...] = _dot(h_scr[cur], w_ref[...])

    @pl.when((j == nj - 1) & (i + 1 < ni))
    def _():
        proj_ref[...] = _dot(h_scr[cur], w_ref[...])
        h_scr[1 - cur] = prologue()

    @pl.when((j == nj - 1) & (i + 1 == ni))
    def _():
        proj_ref[...] = _dot(h_scr[cur], w_ref[...])


def _inproj(x2d, mod3, norm_g, w_all, n_main, w_ba, alog_lanes, dtb_lanes, pool_w, pool_scale, seq):
    T, D = x2d.shape
    G, gd, _ = pool_w.shape
    tm, tn = 1024, 1024
    ni, nj = T // tm, n_main // tn
    assert tm % DN_CHUNK == 0 and seq % tm == 0 and n_main % tn == 0 and G * gd == tn and nj >= 3

    def ahead(i, j):
        return jnp.minimum(i + (j == nj - 1).astype(jnp.int32), ni - 1)

    const = lambda i, j: (0, 0)
    return pl.pallas_call(
        functools.partial(_inproj_kernel, seq=seq, gd=gd),
        grid=(ni, nj),
        in_specs=[pl.BlockSpec((tm, D), lambda i, j: (ahead(i, j), 0)),
                  pl.BlockSpec((1, 6, D), lambda i, j: (ahead(i, j) * tm // seq, 0, 0)),
                  pl.BlockSpec((1, D), const),
                  pl.BlockSpec((D, tn), lambda i, j: (0, j)),
                  pl.BlockSpec((D, LANES), const),
                  pl.BlockSpec((1, LANES), const),
                  pl.BlockSpec((1, LANES), const),
                  pl.BlockSpec((G, gd, gd), lambda i, j: (0, 0, 0)),
                  pl.BlockSpec((1, tn), const)],
        out_specs=[pl.BlockSpec((tm, tn), lambda i, j: (i, jnp.maximum(j - 1, 0))),
                   pl.BlockSpec((tm, LANES), lambda i, j: (ahead(i, j), 0)),
                   pl.BlockSpec((tm, tn), lambda i, j: (i, 0))],
        out_shape=[jax.ShapeDtypeStruct((T, n_main - tn), F32),
                   jax.ShapeDtypeStruct((T, LANES), F32),
                   jax.ShapeDtypeStruct((T, tn), BF16)],
        scratch_shapes=[pltpu.VMEM((2, tm, D), BF16), pltpu.VMEM((tm, tn), F32),
                        pltpu.VMEM((POOL_HALO, tn), F32)],
        compiler_params=_cparams("arbitrary", "arbitrary"),
        name="inproj",
    )(x2d, mod3, norm_g.reshape(1, D), w_all, w_ba, alog_lanes, dtb_lanes, pool_w,
      pool_scale.reshape(1, tn))


def _bmm(a, b):
    return jnp.einsum('nij,njk->nik', a.astype(BF16), b.astype(BF16),
                      preferred_element_type=F32)


def _bmm_nt(a, b):
    return jnp.einsum('nid,njd->nij', a.astype(BF16), b.astype(BF16),
                      preferred_element_type=F32)


def _dn_prepare(q_raw, k_raw, v_raw, cwq, cwk, cwv, beta_col, gc_col, *, chunk):
    seq, d = q_raw.shape
    n = seq // chunk
    top = lax.broadcasted_iota(jnp.int32, (8, d), 0)

    def shift(a, sh):
        r = pltpu.roll(a, sh, axis=0)
        return jnp.concatenate([jnp.where(top >= sh, r[:8], 0.0), r[8:]], axis=0)

    def conv_silu(x, cw):
        assert CONV_WIDTH == 4
        w0, w1, w2, w3 = (cw[j:j + 1, :] for j in range(CONV_WIDTH))
        x1 = shift(x, 1)
        return _silu(x * w3 + x1 * w2 + shift(x * w1 + x1 * w0, 2))

    def l2n(x):
        return x * lax.rsqrt(jnp.sum(x * x, axis=-1, keepdims=True) + EPS)

    qn = l2n(conv_silu(q_raw, cwq)) * (d ** -0.5)
    kn = l2n(conv_silu(k_raw, cwk))
    v = conv_silu(v_raw, cwv)
    beta = jnp.broadcast_to(beta_col, (seq, d))
    gc = jnp.broadcast_to(gc_col, (seq, d))

    gc3 = gc.reshape(n, chunk, d)
    ii = lax.broadcasted_iota(jnp.int32, (chunk, chunk), 0)
    jj = lax.broadcasted_iota(jnp.int32, (chunk, chunk), 1)
    incl = (ii >= jj)[None]
    strict = (ii > jj)[None]
    diff = gc3 - jnp.swapaxes(gc3, 1, 2)
    decay = jnp.where(incl, jnp.exp(jnp.where(incl, diff, 0.0)), 0.0)

    kb = kn * beta
    kn3 = kn.reshape(n, chunk, d)
    lmat = jnp.where(strict, _bmm_nt(kb.reshape(n, chunk, d), kn3) * decay, 0.0)
    qk = jnp.where(incl, _bmm_nt(qn.reshape(n, chunk, d), kn3) * decay, 0.0)

    def merge_mask(lv):
        same = (ii >> (lv + 1)) == (jj >> (lv + 1))
        return (same & (((ii >> lv) & 1) == 1) & (((jj >> lv) & 1) == 0))[None]

    tinv = (ii == jj).astype(F32)[None] - jnp.where(merge_mask(0), lmat, 0.0)
    for lv in range(1, chunk.bit_length() - 1):
        a21 = jnp.where(merge_mask(lv), lmat, 0.0)
        tinv = tinv - _bmm(tinv, _bmm(a21, tinv))

    egc = jnp.exp(gc)
    rhs = jnp.concatenate([v * beta, kb * egc], axis=-1).reshape(n, chunk, 2 * d)
    uw = _bmm(tinv, rhs)
    glast = gc3[:, chunk - 1:chunk, :]
    kdt = jnp.swapaxes(kn3 * jnp.exp(glast - gc3), 1, 2)
    kuw = _bmm(kdt, uw)
    quw = _bmm(qk, uw)
    qp = (qn * egc).reshape(n, chunk, d) - quw[..., d:]
    return quw[..., :d], qp, kuw[..., :d], kuw[..., d:], jnp.exp(glast)


def _dn_kernel(q_ref, k_ref, v_ref, z_ref, gates_ref, cwq_ref, cwk_ref, cwv_ref, on_ref,
               o_ref, o_scr, *, chunk, heads_per_step):
    d = DN_HEAD_DIM
    seq = q_ref.shape[0]
    n = seq // chunk
    lane = lax.broadcasted_iota(jnp.int32, (seq, LANES), 1)
    gates = gates_ref[...]
    prepared = []
    for i in range(heads_per_step):
        head = pl.program_id(1) * heads_per_step + i
        cols = slice(i * d, (i + 1) * d)
        beta_col = jnp.sum(jnp.where(lane == head, gates, 0.0), axis=-1, keepdims=True)
        gc_col = jnp.sum(jnp.where(lane == head + DN_HEADS, gates, 0.0), axis=-1, keepdims=True)
        prepared.append(_dn_prepare(q_ref[:, cols], k_ref[:, cols], v_ref[:, cols],
                                    cwq_ref[:, cols], cwk_ref[:, cols], cwv_ref[:, cols],
                                    beta_col, gc_col, chunk=chunk))

    states = [jnp.zeros((d, d), F32) for _ in range(heads_per_step)]
    for c in range(n):
        for i, (o0, qp, kub, kuw, eglast) in enumerate(prepared):
            sb = states[i].astype(BF16)
            o_scr[c * chunk:(c + 1) * chunk, i * d:(i + 1) * d] = o0[c] + _dot(qp[c].astype(BF16), sb)
            states[i] = states[i] * eglast[c] + kub[c] - _dot(kuw[c].astype(BF16), sb)

    for i in range(heads_per_step):
        cols = slice(i * d, (i + 1) * d)
        o = o_scr[:, cols]
        o = o * lax.rsqrt(jnp.mean(o * o, axis=-1, keepdims=True) + EPS) * on_ref[...]
        o_ref[:, cols] = (o * _silu(z_ref[:, cols])).astype(o_ref.dtype)


def _deltanet(proj, gates, conv_w, o_norm_g, batch, seq, col0):
    T = proj.shape[0]
    d = DN_HEAD_DIM
    H = DN_HEADS
    hps = 2
    w = hps * d
    cb = col0 // w
    nhb = H // hps

    def colspec(off):
        return pl.BlockSpec((seq, w), lambda b, h: (b, off + h))

    def convspec(off):
        return pl.BlockSpec((CONV_WIDTH, w), lambda b, h: (0, off + h))

    return pl.pallas_call(
        functools.partial(_dn_kernel, chunk=DN_CHUNK, heads_per_step=hps),
        grid=(batch, nhb),
        in_specs=[colspec(cb), colspec(cb + nhb), colspec(cb + 2 * nhb), colspec(cb + 3 * nhb),
                  pl.BlockSpec((seq, LANES), lambda b, h: (b, 0)),
                  convspec(0), convspec(nhb), convspec(2 * nhb),
                  pl.BlockSpec((1, d), lambda b, h: (0, 0))],
        out_specs=pl.BlockSpec((seq, w), lambda b, h: (b, h)),
        out_shape=jax.ShapeDtypeStruct((T, H * d), BF16),
        scratch_shapes=[pltpu.VMEM((seq, w), F32)],
        compiler_params=_cparams("arbitrary", "arbitrary"),
        name="deltanet",
    )(proj, proj, proj, proj, gates, conv_w, conv_w, conv_w, o_norm_g.reshape(1, d))


def _route(logits):
    lane = lax.broadcasted_iota(jnp.int32, logits.shape, 1)
    lanef = lane.astype(F32)
    far = float(LANES)

    def first_max(vals):
        m = jnp.max(vals, axis=-1, keepdims=True)
        idx = jnp.min(jnp.where(vals == m, lanef, far), axis=-1, keepdims=True)
        return m, idx

    gl = jnp.where(lane < N_GROUPS, logits, NEG_BIG)
    gmax, gidx = first_max(gl)
    p_top = 1.0 / jnp.sum(jnp.exp(gl - gmax), axis=-1, keepdims=True)
    lo = N_GROUPS + EXPERTS_PER_GROUP * gidx
    el = jnp.where(lanef >= lo, jnp.where(lanef < lo + EXPERTS_PER_GROUP, logits, NEG_BIG), NEG_BIG)
    m1, i1 = first_max(el)
    m2, i2 = first_max(jnp.where(lanef == i1, NEG_BIG, el))
    t = jnp.exp(m2 - m1)
    w1 = p_top / (1.0 + t)
    w2 = w1 * t
    return jnp.where(lane == 0, i1 - N_GROUPS,
                     jnp.where(lane == 1, i2 - N_GROUPS,
                               jnp.where(lane == 2, w1, jnp.where(lane == 3, w2, 0.0))))


def _slab_store(ref, val):
    rows, width = val.shape
    ns = width // LANES
    for s in range(ns):
        ref[pl.ds(s, rows, stride=ns), :] = val[:, s * LANES:(s + 1) * LANES]


def _slab_load(ref, first_row, rows, ns, pitch):
    return [ref[pl.ds(first_row * pitch + s, rows, stride=pitch), :] for s in range(ns)]


def _gather_pitch(ns):
    return ns + 8 if (ns // 8) % 2 == 0 else ns


def _outproj_kernel(yp_ref, yd_ref, wo_ref, x_ref, mod_ref, g_ref, wr_ref, br_ref,
                    x2_ref, h2_ref, route_ref):
    half = yp_ref.shape[1]
    out = _dot(yp_ref[...], wo_ref[:half, :]) + _dot(yd_ref[...], wo_ref[half:, :])
    x2 = x_ref[...] + mod_ref[0, 2:3, :] * out
    x2_ref[...] = x2
    ms = jnp.mean(x2 * x2, axis=-1, keepdims=True)
    y = x2 * lax.rsqrt(ms + EPS) * g_ref[...]
    h2 = y * (1.0 + mod_ref[0, 4:5, :]) + mod_ref[0, 3:4, :]
    _slab_store(h2_ref, h2)
    hi = h2.astype(BF16)
    lo = (h2 - hi.astype(F32)).astype(BF16)
    both = _dot(hi, wr_ref[...])
    logits = both[:, :LANES] + (_dot(lo, wr_ref[:, :LANES]) + both[:, LANES:]) + br_ref[...]
    route_ref[...] = _route(logits)


def _outproj(y_pool, y_dn, w_out_bf, x2d, mod3, norm_g, w_router, b_router, seq):
    T, D = x2d.shape
    half = y_pool.shape[1]
    tm = 512
    row = lambda i: (i, 0)
    const = lambda i: (0, 0)
    return pl.pallas_call(
        _outproj_kernel,
        grid=(T // tm,),
        in_specs=[pl.BlockSpec((tm, half), row),
                  pl.BlockSpec((tm, half), row),
                  pl.BlockSpec((2 * half, D), const, pipeline_mode=pl.Buffered(1)),
                  pl.BlockSpec((tm, D), row),
                  pl.BlockSpec((1, 6, D), lambda i: (i * tm // seq, 0, 0)),
                  pl.BlockSpec((1, D), const),
                  pl.BlockSpec((D, 2 * LANES), const),
                  pl.BlockSpec((1, LANES), const)],
        out_specs=[pl.BlockSpec((tm, D), row),
                   pl.BlockSpec((tm * (D // LANES), LANES), row),
                   pl.BlockSpec((tm, LANES), row)],
        out_shape=[jax.ShapeDtypeStruct((T, D), F32),
                   jax.ShapeDtypeStruct((T * (D // LANES), LANES), F32),
                   jax.ShapeDtypeStruct((T, LANES), F32)],
        compiler_params=_cparams("arbitrary"),
        name="outproj",
    )(y_pool, y_dn, w_out_bf, x2d, mod3, norm_g.reshape(1, D), w_router, b_router)


GATHER_UNROLL = 8


def _row_gather(idx_ref, nrows, ns, pitch, src_hbm, dst, sem, queues=1):
    def body(k, carry):
        for q in range(queues):
            r = k * queues + q
            src_row = pl.multiple_of(idx_ref[0, 0, r] * ns, ns)
            dst_row = pl.multiple_of(r * pitch, 8)
            pltpu.make_async_copy(src_hbm.at[pl.ds(src_row, ns), :],
                                  dst.at[pl.ds(dst_row, ns), :], sem).start(priority=q)
        return carry
    lax.fori_loop(0, nrows // queues, body, 0, unroll=GATHER_UNROLL // queues)


def _row_gather_wait(nrows, ns, src_hbm, dst, sem):
    pltpu.make_async_copy(src_hbm.at[pl.ds(0, nrows * ns), :], dst.at[pl.ds(0, nrows * ns), :],
                          sem).wait()


WEIGHT_DMA_PRIORITY = 1


def _expert_kernel(be_ref, first_ref, wslot_ref, nexte_ref, nact_ref, tok_ref, tokn_ref,
                   h_hbm, wg_hbm, wu_hbm, wd_hbm, ys_ref, xbuf, wg_buf, wu_buf, wd_buf,
                   sem, wsem, *, bm, ns, pitch):
    i = pl.program_id(0)
    nact = nact_ref[0]
    slot = i % 2
    wslot = wslot_ref[i]

    def weight_copies(e, s):
        return (pltpu.make_async_copy(wg_hbm.at[e], wg_buf.at[s], wsem.at[s]),
                pltpu.make_async_copy(wu_hbm.at[e], wu_buf.at[s], wsem.at[s]),
                pltpu.make_async_copy(wd_hbm.at[e], wd_buf.at[s], wsem.at[s]))

    @pl.when(i == 0)
    def _():
        for cp in weight_copies(be_ref[0], 0):
            cp.start(priority=WEIGHT_DMA_PRIORITY)
        _row_gather(tok_ref, bm, ns, pitch, h_hbm, xbuf.at[0], sem.at[0])

    @pl.when((first_ref[i] == 1) & (nexte_ref[i] >= 0))
    def _():
        for cp in weight_copies(nexte_ref[i], 1 - wslot):
            cp.start(priority=WEIGHT_DMA_PRIORITY)

    @pl.when(i + 1 < nact)
    def _():
        _row_gather(tokn_ref, bm, ns, pitch, h_hbm, xbuf.at[1 - slot], sem.at[1 - slot])

    @pl.when(first_ref[i] == 1)
    def _():
        for cp in weight_copies(be_ref[i], wslot):
            cp.wait()

    @pl.when(i < nact)
    def _():
        _row_gather_wait(bm, ns, h_hbm, xbuf.at[slot], sem.at[slot])
        xb = jnp.concatenate(_slab_load(xbuf.at[slot], 0, bm, ns, pitch), axis=-1).astype(BF16)
        gate = _dot(xb, wg_buf[wslot].astype(BF16))
        up = _dot(xb, wu_buf[wslot].astype(BF16))
        hid = (_silu(gate) * up).astype(BF16)
        _slab_store(ys_ref, _dot(hid, wd_buf[wslot].astype(BF16)))

    @pl.when(i >= nact)
    def _():
        ys_ref[...] = jnp.zeros_like(ys_ref)


def _experts(h2_slab, block_e, nact, buf_tok, w_gate, w_up, w_down):
    E, D, De = w_gate.shape
    ns = D // LANES
    pitch = _gather_pitch(ns)
    bm = MOE_BLOCK
    n_pad = buf_tok.shape[0]
    nb = n_pad // bm
    tok3 = buf_tok.reshape(nb, 1, bm)

    idx = jnp.arange(nb, dtype=jnp.int32)
    active = idx < nact[0]
    prev_e = jnp.concatenate([jnp.full((1,), -1, jnp.int32), block_e[:-1]])
    first = (active & (block_e != prev_e)).astype(jnp.int32)
    wslot = ((jnp.cumsum(first) - 1) % 2).astype(jnp.int32)
    later_first = (first[None, :] == 1) & (idx[None, :] > idx[:, None])
    nxt = jnp.min(jnp.where(later_first, idx[None, :], nb), axis=1)
    next_e = jnp.where(nxt < nb, _lookup(block_e, jnp.minimum(nxt, nb - 1)), -1).astype(jnp.int32)

    any_spec = pl.BlockSpec(memory_space=pl.ANY)
    grid_spec = pltpu.PrefetchScalarGridSpec(
        num_scalar_prefetch=5,
        grid=(nb,),
        in_specs=[pl.BlockSpec((1, 1, bm), lambda i, *_: (i, 0, 0), memory_space=pltpu.SMEM),
                  pl.BlockSpec((1, 1, bm), lambda i, *_: (jnp.minimum(i + 1, nb - 1), 0, 0),
                               memory_space=pltpu.SMEM),
                  any_spec, any_spec, any_spec, any_spec],
        out_specs=pl.BlockSpec((bm * ns, LANES), lambda i, *_: (i, 0)),
        scratch_shapes=[pltpu.VMEM((2, bm * pitch, LANES), F32),
                        pltpu.VMEM((2, D, De), F32), pltpu.VMEM((2, D, De), F32),
                        pltpu.VMEM((2, De, D), F32),
                        pltpu.SemaphoreType.DMA((2,)), pltpu.SemaphoreType.DMA((2,))],
    )
    return pl.pallas_call(
        functools.partial(_expert_kernel, bm=bm, ns=ns, pitch=pitch),
        grid_spec=grid_spec,
        out_shape=jax.ShapeDtypeStruct((n_pad * ns, LANES), F32),
        compiler_params=_cparams("arbitrary"),
        name="experts",
    )(block_e, first, wslot, next_e, nact, tok3, tok3, h2_slab, w_gate, w_up, w_down)


def _combine_kernel(pos_ref, posn_ref, ys_hbm, x2_ref, mod_ref, route_ref, g_ref, o_ref,
                    ybuf, sem, *, tm, ns, pitch):
    i = pl.program_id(0)
    nsteps = pl.num_programs(0)
    slot = i % 2
    nrows = TOP_K * tm

    @pl.when(i == 0)
    def _():
        _row_gather(pos_ref, nrows, ns, pitch, ys_hbm, ybuf.at[0], sem.at[0], queues=2)

    @pl.when(i + 1 < nsteps)
    def _():
        _row_gather(posn_ref, nrows, ns, pitch, ys_hbm, ybuf.at[1 - slot], sem.at[1 - slot], queues=2)

    _row_gather_wait(nrows, ns, ys_hbm, ybuf.at[slot], sem.at[slot])
    route = route_ref[...]
    w0 = route[:, 2:3]
    w1 = route[:, 3:4]
    y0 = _slab_load(ybuf.at[slot], 0, tm, ns, pitch)
    y1 = _slab_load(ybuf.at[slot], tm, tm, ns, pitch)
    y = jnp.concatenate([w0 * a + w1 * b for a, b in zip(y0, y1)], axis=-1)
    x3 = x2_ref[...] + mod_ref[0, 5:6, :] * y
    ms = jnp.mean(x3 * x3, axis=-1, keepdims=True)
    o_ref[...] = x3 * lax.rsqrt(ms + EPS) * g_ref[...]


def _combine(ys_slab, pos, x2, mod3, route, norm_g, seq):
    T, D = x2.shape
    ns = D // LANES
    pitch = _gather_pitch(ns)
    tm = 256
    nt = T // tm
    pos3 = pos.reshape(nt, tm, TOP_K).transpose(0, 2, 1).reshape(nt, 1, TOP_K * tm)
    row = lambda i: (i, 0)
    return pl.pallas_call(
        functools.partial(_combine_kernel, tm=tm, ns=ns, pitch=pitch),
        grid=(nt,),
        in_specs=[pl.BlockSpec((1, 1, TOP_K * tm), lambda i: (i, 0, 0), memory_space=pltpu.SMEM),
                  pl.BlockSpec((1, 1, TOP_K * tm), lambda i: (jnp.minimum(i + 1, nt - 1), 0, 0),
                               memory_space=pltpu.SMEM),
                  pl.BlockSpec(memory_space=pl.ANY),
                  pl.BlockSpec((tm, D), row),
                  pl.BlockSpec((1, 6, D), lambda i: (i * tm // seq, 0, 0)),
                  pl.BlockSpec((tm, LANES), row),
                  pl.BlockSpec((1, D), lambda i: (0, 0))],
        out_specs=pl.BlockSpec((tm, D), row),
        out_shape=jax.ShapeDtypeStruct((T, D), F32),
        scratch_shapes=[pltpu.VMEM((2, TOP_K * tm * pitch, LANES), F32),
                        pltpu.SemaphoreType.DMA((2,))],
        compiler_params=_cparams("arbitrary"),
        name="combine",
    )(pos3, pos3, ys_slab, x2, mod3, route, norm_g.reshape(1, D))


def _lookup(table, idx):
    hit = idx[..., None] == jnp.arange(table.shape[0], dtype=idx.dtype)
    return jnp.sum(jnp.where(hit, table, 0), axis=-1)


def _dispatch_plan(route, n_tokens):
    A = n_tokens * TOP_K
    bm = MOE_BLOCK
    nb = (A + N_EXPERTS * (bm - 1)) // bm + 1
    flat_e = route[:, :TOP_K].astype(jnp.int32).reshape(A)
    iota = jnp.arange(A, dtype=jnp.int32)
    _, order = lax.sort_key_val(flat_e, iota)
    _, inv = lax.sort_key_val(order, iota)
    onehot = flat_e[:, None] == jnp.arange(N_EXPERTS, dtype=jnp.int32)[None, :]
    counts = lax.optimization_barrier(jnp.sum(onehot, axis=0, dtype=jnp.int32))
    padded = (counts + bm - 1) // bm * bm
    pad_end = jnp.cumsum(padded)
    pad_start = pad_end - padded
    start = jnp.cumsum(counts) - counts
    pos = inv + jnp.sum(jnp.where(onehot, (pad_start - start)[None, :], 0), axis=1)
    block_start = jnp.arange(nb, dtype=jnp.int32) * bm
    block_e = jnp.minimum(jnp.sum(pad_end[None, :] <= block_start[:, None], axis=1),
                          N_EXPERTS - 1).astype(jnp.int32)
    r = (block_start - _lookup(pad_start, block_e))[:, None] + jnp.arange(bm, dtype=jnp.int32)[None, :]
    src = jnp.clip(_lookup(start, block_e)[:, None] + r, 0, A - 1)
    buf_tok = jnp.where(r < _lookup(counts, block_e)[:, None], order[src] // TOP_K, 0)
    nact = (pad_end[-1:] // bm).astype(jnp.int32)
    return buf_tok.reshape(nb * bm), pos.astype(jnp.int32), block_e, nact


def kernel(x, c, w_ada, b_ada, norm1_g, w_in, pool_w, pool_scale, conv_w, a_log, dt_bias,
           o_norm_g, w_out, norm2_g, w_router_group, b_router_group, w_router_expert,
           b_router_expert, w_gate, w_up, w_down, norm_f_g):
    B, S, D = x.shape
    T = B * S
    depth = w_ada.shape[0]
    pool_width = pool_w.shape[1] * pool_w.shape[2]
    n_main = pool_width + 4 * DN_HEADS * DN_HEAD_DIM
    n_route = N_GROUPS + N_EXPERTS

    assert depth == 1, "kernel supports the single-layer configuration only"
    l = 0
    xt = x.reshape(T, D)
    mod3 = _ada(c, w_ada[l], b_ada[l]).reshape(B, 6, D)

    w_all = w_in[l].astype(BF16)
    w_ba = jnp.pad(w_in[l, :, n_main:], ((0, 0), (0, LANES - 2 * DN_HEADS))).astype(BF16)
    gate_pad = (DN_HEADS, LANES - 2 * DN_HEADS)
    alog_lanes = jnp.pad(a_log[l], gate_pad).reshape(1, LANES)
    dtb_lanes = jnp.pad(dt_bias[l], gate_pad).reshape(1, LANES)
    proj, gates, y_pool = _inproj(xt, mod3, norm1_g[l], w_all, n_main, w_ba, alog_lanes, dtb_lanes,
                                  pool_w[l], pool_scale[l], S)
    y_dn = _deltanet(proj, gates, conv_w[l], o_norm_g[l], B, S, 0)

    w_router = jnp.pad(jnp.concatenate([w_router_group[l], w_router_expert[l]], axis=1),
                       ((0, 0), (0, LANES - n_route)))
    b_router = jnp.pad(jnp.concatenate([b_router_group[l], b_router_expert[l]]),
                       (0, LANES - n_route)).reshape(1, LANES)
    w_router_hi = w_router.astype(BF16)
    w_router_lo = (w_router - w_router_hi.astype(F32)).astype(BF16)
    x2, h2, route = _outproj(y_pool, y_dn, w_out[l].astype(BF16), xt, mod3, norm2_g[l],
                             jnp.concatenate([w_router_hi, w_router_lo], axis=1), b_router, S)

    buf_tok, pos, block_e, nact = _dispatch_plan(route, T)
    ys = _experts(h2, block_e, nact, buf_tok, w_gate[l], w_up[l], w_down[l])
    out = _combine(ys, pos, x2, mod3, route, norm_f_g, S)
    return out.reshape(B, S, D)
```

```python
import functools

import jax
import jax.numpy as jnp
from jax import lax
from jax.experimental import pallas as pl
from jax.experimental.pallas import tpu as pltpu

F32 = jnp.float32
BF16 = jnp.bfloat16

POOL_GROUPS = 4
POOL_WINDOWS = (2, 4, 8, 16)
POOL_HALO = 16
DN_HEADS = 8
DN_HEAD_DIM = 128
CONV_WIDTH = 4
DN_CHUNK = 128
N_GROUPS = 4
EXPERTS_PER_GROUP = 8
N_EXPERTS = N_GROUPS * EXPERTS_PER_GROUP
TOP_K = 2
MOE_BLOCK = 256
EPS = 1e-6
LANES = 128
NEG_BIG = -3.0e38
VMEM_LIMIT = 60 * 1024 * 1024


def _silu(x):
    half = 0.5 * x
    return half * (1.0 + jnp.tanh(half))


def _chunk_cumsum(x, chunk):
    rmod = lax.broadcasted_iota(jnp.int32, x.shape, 0) & (chunk - 1)
    k = 1
    while k < chunk:
        x = x + jnp.where(rmod >= k, pltpu.roll(x, k, axis=0), 0.0)
        k *= 2
    return x


def _dot(a, b):
    return jnp.dot(a, b, preferred_element_type=F32)


def _cparams(*sem):
    return pltpu.CompilerParams(dimension_semantics=sem, vmem_limit_bytes=VMEM_LIMIT)


def _ada_kernel(c_ref, w_ref, b_ref, o_ref):
    ca = _silu(c_ref[...])
    nb = ca.shape[0]
    c_hi = ca.astype(BF16).astype(F32)
    hi_lo = jnp.concatenate([c_hi, ca - c_hi], axis=0).astype(BF16)
    w = w_ref[...]
    w_hi = w.astype(BF16)
    w_lo = (w - w_hi.astype(F32)).astype(BF16)
    both = _dot(hi_lo, w_hi)
    o_ref[...] = both[:nb] + (both[nb:] + _dot(c_hi.astype(BF16), w_lo)) + b_ref[...]


def _ada(c, w_ada, b_ada):
    B, D = c.shape
    N = w_ada.shape[1]
    tn = 1024
    return pl.pallas_call(
        _ada_kernel,
        grid=(N // tn,),
        in_specs=[pl.BlockSpec((B, D), lambda j: (0, 0)),
                  pl.BlockSpec((D, tn), lambda j: (0, j)),
                  pl.BlockSpec((1, tn), lambda j: (0, j))],
        out_specs=pl.BlockSpec((B, tn), lambda j: (0, j)),
        out_shape=jax.ShapeDtypeStruct((B, N), F32),
        compiler_params=_cparams("arbitrary"),
        name="ada",
    )(c, w_ada, b_ada.reshape(1, N))


def _pool_mix(u, halo, pos0, w_ref, sc_ref, o_ref, gd):
    rows = u.shape[0]
    tpos = lax.broadcasted_iota(jnp.int32, (rows, 1), 0) + pos0 + 1
    for g in range(POOL_GROUPS):
        win = POOL_WINDOWS[g]
        cols = slice(g * gd, (g + 1) * gd)
        s = jnp.concatenate([halo[:, cols], u[:, cols]], axis=0)
        k = 1
        while k < win:
            s = s + pltpu.roll(s, k, axis=0)
            k *= 2
        cnt = jnp.minimum(tpos, win).astype(F32)
        diff = s[POOL_HALO:, :] / cnt - u[:, cols]
        y = _dot(diff.astype(BF16), w_ref[g].astype(BF16)) * sc_ref[:, cols]
        o_ref[:, cols] = y.astype(o_ref.dtype)


def _inproj_kernel(x_ref, mod_ref, g_ref, w_ref, wba_ref, alog_ref, dtb_ref, pw_ref, psc_ref,
                   proj_ref, gates_ref, ypool_ref, h_scr, u_scr, halo_scr, *, seq, gd):
    i = pl.program_id(0)
    j = pl.program_id(1)
    ni = pl.num_programs(0)
    nj = pl.num_programs(1)
    tm = x_ref.shape[0]
    cur = i % 2

    def prologue():
        x = x_ref[...]
        ms = jnp.mean(x * x, axis=-1, keepdims=True)
        y = x * lax.rsqrt(ms + EPS) * g_ref[...]
        h = (y * (1.0 + mod_ref[0, 1:2, :]) + mod_ref[0, 0:1, :]).astype(BF16)
        ba = _dot(h, wba_ref[...])
        lane = lax.broadcasted_iota(jnp.int32, ba.shape, 1)
        a = ba + dtb_ref[...]
        softplus = jnp.maximum(a, 0.0) + jnp.log1p(jnp.exp(-jnp.abs(a)))
        gc = _chunk_cumsum(-jnp.exp(alog_ref[...]) * softplus, DN_CHUNK)
        gates_ref[...] = jnp.where(lane < DN_HEADS, jax.nn.sigmoid(ba), gc)
        return h

    @pl.when((i == 0) & (j == 0))
    def _():
        h = prologue()
        h_scr[0] = h
        halo_scr[...] = jnp.zeros_like(halo_scr)
        u_scr[...] = _dot(h, w_ref[...])

    @pl.when((i > 0) & (j == 0))
    def _():
        u_scr[...] = _dot(h_scr[cur], w_ref[...])

    @pl.when(j == 1)
    def _():
        proj_ref[...] = _dot(h_scr[cur], w_ref[...])
        pos0 = (i * tm) % seq
        u = u_scr[...]
        halo = jnp.where(pos0 > 0, halo_scr[...], 0.0)
        _pool_mix(u, halo, pos0, pw_ref, psc_ref, ypool_ref, gd)
        halo_scr[...] = u[tm - POOL_HALO:, :]

    @pl.when((j > 1) & (j < nj - 1))
    def _():
        proj_ref[...] = _dot(h_scr[cur], w_ref[...])

    @pl.when((j == nj - 1) & (i + 1 < ni))
    def _():
        proj_ref[...] = _dot(h_scr[cur], w_ref[...])
        h_scr[1 - cur] = prologue()

    @pl.when((j == nj - 1) & (i + 1 == ni))
    def _():
        proj_ref[...] = _dot(h_scr[cur], w_ref[...])


def _inproj(x2d, mod3, norm_g, w_all, n_main, w_ba, alog_lanes, dtb_lanes, pool_w, pool_scale, seq):
    T, D = x2d.shape
    G, gd, _ = pool_w.shape
    tm, tn = 1024, 1024
    ni, nj = T // tm, n_main // tn
    assert tm % DN_CHUNK == 0 and seq % tm == 0 and n_main % tn == 0 and G * gd == tn and nj >= 3

    def ahead(i, j):
        return jnp.minimum(i + (j == nj - 1).astype(jnp.int32), ni - 1)

    const = lambda i, j: (0, 0)
    return pl.pallas_call(
        functools.partial(_inproj_kernel, seq=seq, gd=gd),
        grid=(ni, nj),
        in_specs=[pl.BlockSpec((tm, D), lambda i, j: (ahead(i, j), 0)),
                  pl.BlockSpec((1, 6, D), lambda i, j: (ahead(i, j) * tm // seq, 0, 0)),
                  pl.BlockSpec((1, D), const),
                  pl.BlockSpec((D, tn), lambda i, j: (0, j)),
                  pl.BlockSpec((D, LANES), const),
                  pl.BlockSpec((1, LANES), const),
                  pl.BlockSpec((1, LANES), const),
                  pl.BlockSpec((G, gd, gd), lambda i, j: (0, 0, 0)),
                  pl.BlockSpec((1, tn), const)],
        out_specs=[pl.BlockSpec((tm, tn), lambda i, j: (i, jnp.maximum(j - 1, 0))),
                   pl.BlockSpec((tm, LANES), lambda i, j: (ahead(i, j), 0)),
                   pl.BlockSpec((tm, tn), lambda i, j: (i, 0))],
        out_shape=[jax.ShapeDtypeStruct((T, n_main - tn), F32),
                   jax.ShapeDtypeStruct((T, LANES), F32),
                   jax.ShapeDtypeStruct((T, tn), BF16)],
        scratch_shapes=[pltpu.VMEM((2, tm, D), BF16), pltpu.VMEM((tm, tn), F32),
                        pltpu.VMEM((POOL_HALO, tn), F32)],
        compiler_params=_cparams("arbitrary", "arbitrary"),
        name="inproj",
    )(x2d, mod3, norm_g.reshape(1, D), w_all, w_ba, alog_lanes, dtb_lanes, pool_w,
      pool_scale.reshape(1, tn))


def _bmm(a, b):
    return jnp.einsum('nij,njk->nik', a.astype(BF16), b.astype(BF16),
                      preferred_element_type=F32)


def _bmm_nt(a, b):
    return jnp.einsum('nid,njd->nij', a.astype(BF16), b.astype(BF16),
                      preferred_element_type=F32)


def _dn_prepare(q_raw, k_raw, v_raw, cwq, cwk, cwv, beta_col, gc_col, *, chunk):
    seq, d = q_raw.shape
    n = seq // chunk
    top = lax.broadcasted_iota(jnp.int32, (8, d), 0)

    def shift(a, sh):
        r = pltpu.roll(a, sh, axis=0)
        return jnp.concatenate([jnp.where(top >= sh, r[:8], 0.0), r[8:]], axis=0)

    def conv_silu(x, cw):
        assert CONV_WIDTH == 4
        w0, w1, w2, w3 = (cw[j:j + 1, :] for j in range(CONV_WIDTH))
        x1 = shift(x, 1)
        return _silu(x * w3 + x1 * w2 + shift(x * w1 + x1 * w0, 2))

    def l2n(x):
        return x * lax.rsqrt(jnp.sum(x * x, axis=-1, keepdims=True) + EPS)

    qn = l2n(conv_silu(q_raw, cwq)) * (d ** -0.5)
    kn = l2n(conv_silu(k_raw, cwk))
    v = conv_silu(v_raw, cwv)
    beta = jnp.broadcast_to(beta_col, (seq, d))
    gc = jnp.broadcast_to(gc_col, (seq, d))

    gc3 = gc.reshape(n, chunk, d)
    ii = lax.broadcasted_iota(jnp.int32, (chunk, chunk), 0)
    jj = lax.broadcasted_iota(jnp.int32, (chunk, chunk), 1)
    incl = (ii >= jj)[None]
    strict = (ii > jj)[None]
    diff = gc3 - jnp.swapaxes(gc3, 1, 2)
    decay = jnp.where(incl, jnp.exp(jnp.where(incl, diff, 0.0)), 0.0)

    kb = kn * beta
    kn3 = kn.reshape(n, chunk, d)
    lmat = jnp.where(strict, _bmm_nt(kb.reshape(n, chunk, d), kn3) * decay, 0.0)
    qk = jnp.where(incl, _bmm_nt(qn.reshape(n, chunk, d), kn3) * decay, 0.0)

    def merge_mask(lv):
        same = (ii >> (lv + 1)) == (jj >> (lv + 1))
        return (same & (((ii >> lv) & 1) == 1) & (((jj >> lv) & 1) == 0))[None]

    tinv = (ii == jj).astype(F32)[None] - jnp.where(merge_mask(0), lmat, 0.0)
    for lv in range(1, chunk.bit_length() - 1):
        a21 = jnp.where(merge_mask(lv), lmat, 0.0)
        tinv = tinv - _bmm(tinv, _bmm(a21, tinv))

    egc = jnp.exp(gc)
    rhs = jnp.concatenate([v * beta, kb * egc], axis=-1).reshape(n, chunk, 2 * d)
    uw = _bmm(tinv, rhs)
    glast = gc3[:, chunk - 1:chunk, :]
    kdt = jnp.swapaxes(kn3 * jnp.exp(glast - gc3), 1, 2)
    kuw = _bmm(kdt, uw)
    quw = _bmm(qk, uw)
    qp = (qn * egc).reshape(n, chunk, d) - quw[..., d:]
    return quw[..., :d], qp, kuw[..., :d], kuw[..., d:], jnp.exp(glast)


def _dn_kernel(q_ref, k_ref, v_ref, z_ref, gates_ref, cwq_ref, cwk_ref, cwv_ref, on_ref,
               o_ref, o_scr, *, chunk, heads_per_step):
    d = DN_HEAD_DIM
    seq = q_ref.shape[0]
    n = seq // chunk
    lane = lax.broadcasted_iota(jnp.int32, (seq, LANES), 1)
    gates = gates_ref[...]
    prepared = []
    for i in range(heads_per_step):
        head = pl.program_id(1) * heads_per_step + i
        cols = slice(i * d, (i + 1) * d)
        beta_col = jnp.sum(jnp.where(lane == head, gates, 0.0), axis=-1, keepdims=True)
        gc_col = jnp.sum(jnp.where(lane == head + DN_HEADS, gates, 0.0), axis=-1, keepdims=True)
        prepared.append(_dn_prepare(q_ref[:, cols], k_ref[:, cols], v_ref[:, cols],
                                    cwq_ref[:, cols], cwk_ref[:, cols], cwv_ref[:, cols],
                                    beta_col, gc_col, chunk=chunk))

    states = [jnp.zeros((d, d), F32) for _ in range(heads_per_step)]
    for c in range(n):
        for i, (o0, qp, kub, kuw, eglast) in enumerate(prepared):
            sb = states[i].astype(BF16)
            o_scr[c * chunk:(c + 1) * chunk, i * d:(i + 1) * d] = o0[c] + _dot(qp[c].astype(BF16), sb)
            states[i] = states[i] * eglast[c] + kub[c] - _dot(kuw[c].astype(BF16), sb)

    for i in range(heads_per_step):
        cols = slice(i * d, (i + 1) * d)
        o = o_scr[:, cols]
        o = o * lax.rsqrt(jnp.mean(o * o, axis=-1, keepdims=True) + EPS) * on_ref[...]
        o_ref[:, cols] = (o * _silu(z_ref[:, cols])).astype(o_ref.dtype)


def _deltanet(proj, gates, conv_w, o_norm_g, batch, seq, col0):
    T = proj.shape[0]
    d = DN_HEAD_DIM
    H = DN_HEADS
    hps = 2
    w = hps * d
    cb = col0 // w
    nhb = H // hps

    def colspec(off):
        return pl.BlockSpec((seq, w), lambda b, h: (b, off + h))

    def convspec(off):
        return pl.BlockSpec((CONV_WIDTH, w), lambda b, h: (0, off + h))

    return pl.pallas_call(
        functools.partial(_dn_kernel, chunk=DN_CHUNK, heads_per_step=hps),
        grid=(batch, nhb),
        in_specs=[colspec(cb), colspec(cb + nhb), colspec(cb + 2 * nhb), colspec(cb + 3 * nhb),
                  pl.BlockSpec((seq, LANES), lambda b, h: (b, 0)),
                  convspec(0), convspec(nhb), convspec(2 * nhb),
                  pl.BlockSpec((1, d), lambda b, h: (0, 0))],
        out_specs=pl.BlockSpec((seq, w), lambda b, h: (b, h)),
        out_shape=jax.ShapeDtypeStruct((T, H * d), BF16),
        scratch_shapes=[pltpu.VMEM((seq, w), F32)],
        compiler_params=_cparams("arbitrary", "arbitrary"),
        name="deltanet",
    )(proj, proj, proj, proj, gates, conv_w, conv_w, conv_w, o_norm_g.reshape(1, d))


def _route(logits):
    lane = lax.broadcasted_iota(jnp.int32, logits.shape, 1)
    lanef = lane.astype(F32)
    far = float(LANES)

    def first_max(vals):
        m = jnp.max(vals, axis=-1, keepdims=True)
        idx = jnp.min(jnp.where(vals == m, lanef, far), axis=-1, keepdims=True)
        return m, idx

    gl = jnp.where(lane < N_GROUPS, logits, NEG_BIG)
    gmax, gidx = first_max(gl)
    p_top = 1.0 / jnp.sum(jnp.exp(gl - gmax), axis=-1, keepdims=True)
    lo = N_GROUPS + EXPERTS_PER_GROUP * gidx
    el = jnp.where(lanef >= lo, jnp.where(lanef < lo + EXPERTS_PER_GROUP, logits, NEG_BIG), NEG_BIG)
    m1, i1 = first_max(el)
    m2, i2 = first_max(jnp.where(lanef == i1, NEG_BIG, el))
    t = jnp.exp(m2 - m1)
    w1 = p_top / (1.0 + t)
    w2 = w1 * t
    return jnp.where(lane == 0, i1 - N_GROUPS,
                     jnp.where(lane == 1, i2 - N_GROUPS,
                               jnp.where(lane == 2, w1, jnp.where(lane == 3, w2, 0.0))))


def _slab_store(ref, val):
    rows, width = val.shape
    ns = width // LANES
    for s in range(ns):
        ref[pl.ds(s, rows, stride=ns), :] = val[:, s * LANES:(s + 1) * LANES]


def _slab_load(ref, first_row, rows, ns, pitch):
    return [ref[pl.ds(first_row * pitch + s, rows, stride=pitch), :] for s in range(ns)]


def _gather_pitch(ns):
    return ns + 8 if (ns // 8) % 2 == 0 else ns


def _outproj_kernel(yp_ref, yd_ref, wo_ref, x_ref, mod_ref, g_ref, wr_ref, br_ref,
                    x2_ref, h2_ref, route_ref):
    half = yp_ref.shape[1]
    out = _dot(yp_ref[...], wo_ref[:half, :]) + _dot(yd_ref[...], wo_ref[half:, :])
    x2 = x_ref[...] + mod_ref[0, 2:3, :] * out
    x2_ref[...] = x2
    ms = jnp.mean(x2 * x2, axis=-1, keepdims=True)
    y = x2 * lax.rsqrt(ms + EPS) * g_ref[...]
    h2 = y * (1.0 + mod_ref[0, 4:5, :]) + mod_ref[0, 3:4, :]
    _slab_store(h2_ref, h2)
    hi = h2.astype(BF16)
    lo = (h2 - hi.astype(F32)).astype(BF16)
    both = _dot(hi, wr_ref[...])
    logits = both[:, :LANES] + (_dot(lo, wr_ref[:, :LANES]) + both[:, LANES:]) + br_ref[...]
    route_ref[...] = _route(logits)


def _outproj(y_pool, y_dn, w_out_bf, x2d, mod3, norm_g, w_router, b_router, seq):
    T, D = x2d.shape
    half = y_pool.shape[1]
    tm = 512
    row = lambda i: (i, 0)
    const = lambda i: (0, 0)
    return pl.pallas_call(
        _outproj_kernel,
        grid=(T // tm,),
        in_specs=[pl.BlockSpec((tm, half), row),
                  pl.BlockSpec((tm, half), row),
                  pl.BlockSpec((2 * half, D), const, pipeline_mode=pl.Buffered(1)),
                  pl.BlockSpec((tm, D), row),
                  pl.BlockSpec((1, 6, D), lambda i: (i * tm // seq, 0, 0)),
                  pl.BlockSpec((1, D), const),
                  pl.BlockSpec((D, 2 * LANES), const),
                  pl.BlockSpec((1, LANES), const)],
        out_specs=[pl.BlockSpec((tm, D), row),
                   pl.BlockSpec((tm * (D // LANES), LANES), row),
                   pl.BlockSpec((tm, LANES), row)],
        out_shape=[jax.ShapeDtypeStruct((T, D), F32),
                   jax.ShapeDtypeStruct((T * (D // LANES), LANES), F32),
                   jax.ShapeDtypeStruct((T, LANES), F32)],
        compiler_params=_cparams("arbitrary"),
        name="outproj",
    )(y_pool, y_dn, w_out_bf, x2d, mod3, norm_g.reshape(1, D), w_router, b_router)


GATHER_UNROLL = 8


def _row_gather(idx_ref, nrows, ns, pitch, src_hbm, dst, sem, queues=1):
    def body(k, carry):
        for q in range(queues):
            r = k * queues + q
            src_row = pl.multiple_of(idx_ref[0, 0, r] * ns, ns)
            dst_row = pl.multiple_of(r * pitch, 8)
            pltpu.make_async_copy(src_hbm.at[pl.ds(src_row, ns), :],
                                  dst.at[pl.ds(dst_row, ns), :], sem).start(priority=q)
        return carry
    lax.fori_loop(0, nrows // queues, body, 0, unroll=GATHER_UNROLL // queues)


def _row_gather_wait(nrows, ns, src_hbm, dst, sem):
    pltpu.make_async_copy(src_hbm.at[pl.ds(0, nrows * ns), :], dst.at[pl.ds(0, nrows * ns), :],
                          sem).wait()


WEIGHT_DMA_PRIORITY = 1


def _expert_kernel(be_ref, first_ref, wslot_ref, nexte_ref, nact_ref, tok_ref, tokn_ref,
                   h_hbm, wg_hbm, wu_hbm, wd_hbm, ys_ref, xbuf, wg_buf, wu_buf, wd_buf,
                   sem, wsem, *, bm, ns, pitch):
    i = pl.program_id(0)
    nact = nact_ref[0]
    slot = i % 2
    wslot = wslot_ref[i]

    def weight_copies(e, s):
        return (pltpu.make_async_copy(wg_hbm.at[e], wg_buf.at[s], wsem.at[s]),
                pltpu.make_async_copy(wu_hbm.at[e], wu_buf.at[s], wsem.at[s]),
                pltpu.make_async_copy(wd_hbm.at[e], wd_buf.at[s], wsem.at[s]))

    @pl.when(i == 0)
    def _():
        for cp in weight_copies(be_ref[0], 0):
            cp.start(priority=WEIGHT_DMA_PRIORITY)
        _row_gather(tok_ref, bm, ns, pitch, h_hbm, xbuf.at[0], sem.at[0])

    @pl.when((first_ref[i] == 1) & (nexte_ref[i] >= 0))
    def _():
        for cp in weight_copies(nexte_ref[i], 1 - wslot):
            cp.start(priority=WEIGHT_DMA_PRIORITY)

    @pl.when(i + 1 < nact)
    def _():
        _row_gather(tokn_ref, bm, ns, pitch, h_hbm, xbuf.at[1 - slot], sem.at[1 - slot], queues=2)

    @pl.when(first_ref[i] == 1)
    def _():
        for cp in weight_copies(be_ref[i], wslot):
            cp.wait()

    @pl.when(i < nact)
    def _():
        _row_gather_wait(bm, ns, h_hbm, xbuf.at[slot], sem.at[slot])
        xb = jnp.concatenate(_slab_load(xbuf.at[slot], 0, bm, ns, pitch), axis=-1).astype(BF16)
        gate = _dot(xb, wg_buf[wslot].astype(BF16))
        up = _dot(xb, wu_buf[wslot].astype(BF16))
        hid = (_silu(gate) * up).astype(BF16)
        _slab_store(ys_ref, _dot(hid, wd_buf[wslot].astype(BF16)))

    @pl.when(i >= nact)
    def _():
        ys_ref[...] = jnp.zeros_like(ys_ref)


def _experts(h2_slab, block_e, nact, buf_tok, w_gate, w_up, w_down):
    E, D, De = w_gate.shape
    ns = D // LANES
    pitch = _gather_pitch(ns)
    bm = MOE_BLOCK
    n_pad = buf_tok.shape[0]
    nb = n_pad // bm
    tok3 = buf_tok.reshape(nb, 1, bm)

    idx = jnp.arange(nb, dtype=jnp.int32)
    active = idx < nact[0]
    prev_e = jnp.concatenate([jnp.full((1,), -1, jnp.int32), block_e[:-1]])
    first = (active & (block_e != prev_e)).astype(jnp.int32)
    wslot = ((jnp.cumsum(first) - 1) % 2).astype(jnp.int32)
    later_first = (first[None, :] == 1) & (idx[None, :] > idx[:, None])
    nxt = jnp.min(jnp.where(later_first, idx[None, :], nb), axis=1)
    next_e = jnp.where(nxt < nb, _lookup(block_e, jnp.minimum(nxt, nb - 1)), -1).astype(jnp.int32)

    any_spec = pl.BlockSpec(memory_space=pl.ANY)
    grid_spec = pltpu.PrefetchScalarGridSpec(
        num_scalar_prefetch=5,
        grid=(nb,),
        in_specs=[pl.BlockSpec((1, 1, bm), lambda i, *_: (i, 0, 0), memory_space=pltpu.SMEM),
                  pl.BlockSpec((1, 1, bm), lambda i, *_: (jnp.minimum(i + 1, nb - 1), 0, 0),
                               memory_space=pltpu.SMEM),
                  any_spec, any_spec, any_spec, any_spec],
        out_specs=pl.BlockSpec((bm * ns, LANES), lambda i, *_: (i, 0)),
        scratch_shapes=[pltpu.VMEM((2, bm * pitch, LANES), F32),
                        pltpu.VMEM((2, D, De), F32), pltpu.VMEM((2, D, De), F32),
                        pltpu.VMEM((2, De, D), F32),
                        pltpu.SemaphoreType.DMA((2,)), pltpu.SemaphoreType.DMA((2,))],
    )
    return pl.pallas_call(
        functools.partial(_expert_kernel, bm=bm, ns=ns, pitch=pitch),
        grid_spec=grid_spec,
        out_shape=jax.ShapeDtypeStruct((n_pad * ns, LANES), F32),
        compiler_params=_cparams("arbitrary"),
        name="experts",
    )(block_e, first, wslot, next_e, nact, tok3, tok3, h2_slab, w_gate, w_up, w_down)


def _combine_kernel(pos_ref, posn_ref, ys_hbm, x2_ref, mod_ref, route_ref, g_ref, o_ref,
                    ybuf, sem, *, tm, ns, pitch):
    i = pl.program_id(0)
    nsteps = pl.num_programs(0)
    slot = i % 2
    nrows = TOP_K * tm

    @pl.when(i == 0)
    def _():
        _row_gather(pos_ref, nrows, ns, pitch, ys_hbm, ybuf.at[0], sem.at[0], queues=2)

    @pl.when(i + 1 < nsteps)
    def _():
        _row_gather(posn_ref, nrows, ns, pitch, ys_hbm, ybuf.at[1 - slot], sem.at[1 - slot], queues=2)

    _row_gather_wait(nrows, ns, ys_hbm, ybuf.at[slot], sem.at[slot])
    route = route_ref[...]
    w0 = route[:, 2:3]
    w1 = route[:, 3:4]
    y0 = _slab_load(ybuf.at[slot], 0, tm, ns, pitch)
    y1 = _slab_load(ybuf.at[slot], tm, tm, ns, pitch)
    y = jnp.concatenate([w0 * a + w1 * b for a, b in zip(y0, y1)], axis=-1)
    x3 = x2_ref[...] + mod_ref[0, 5:6, :] * y
    ms = jnp.mean(x3 * x3, axis=-1, keepdims=True)
    o_ref[...] = x3 * lax.rsqrt(ms + EPS) * g_ref[...]


def _combine(ys_slab, pos, x2, mod3, route, norm_g, seq):
    T, D = x2.shape
    ns = D // LANES
    pitch = _gather_pitch(ns)
    tm = 256
    nt = T // tm
    pos3 = pos.reshape(nt, tm, TOP_K).transpose(0, 2, 1).reshape(nt, 1, TOP_K * tm)
    row = lambda i: (i, 0)
    return pl.pallas_call(
        functools.partial(_combine_kernel, tm=tm, ns=ns, pitch=pitch),
        grid=(nt,),
        in_specs=[pl.BlockSpec((1, 1, TOP_K * tm), lambda i: (i, 0, 0), memory_space=pltpu.SMEM),
                  pl.BlockSpec((1, 1, TOP_K * tm), lambda i: (jnp.minimum(i + 1, nt - 1), 0, 0),
                               memory_space=pltpu.SMEM),
                  pl.BlockSpec(memory_space=pl.ANY),
                  pl.BlockSpec((tm, D), row),
                  pl.BlockSpec((1, 6, D), lambda i: (i * tm // seq, 0, 0)),
                  pl.BlockSpec((tm, LANES), row),
                  pl.BlockSpec((1, D), lambda i: (0, 0))],
        out_specs=pl.BlockSpec((tm, D), row),
        out_shape=jax.ShapeDtypeStruct((T, D), F32),
        scratch_shapes=[pltpu.VMEM((2, TOP_K * tm * pitch, LANES), F32),
                        pltpu.SemaphoreType.DMA((2,))],
        compiler_params=_cparams("arbitrary"),
        name="combine",
    )(pos3, pos3, ys_slab, x2, mod3, route, norm_g.reshape(1, D))


def _lookup(table, idx):
    hit = idx[..., None] == jnp.arange(table.shape[0], dtype=idx.dtype)
    return jnp.sum(jnp.where(hit, table, 0), axis=-1)


def _dispatch_plan(route, n_tokens):
    A = n_tokens * TOP_K
    bm = MOE_BLOCK
    nb = (A + N_EXPERTS * (bm - 1)) // bm + 1
    flat_e = route[:, :TOP_K].astype(jnp.int32).reshape(A)
    iota = jnp.arange(A, dtype=jnp.int32)
    _, order = lax.sort_key_val(flat_e, iota)
    _, inv = lax.sort_key_val(order, iota)
    onehot = flat_e[:, None] == jnp.arange(N_EXPERTS, dtype=jnp.int32)[None, :]
    counts = lax.optimization_barrier(jnp.sum(onehot, axis=0, dtype=jnp.int32))
    padded = (counts + bm - 1) // bm * bm
    pad_end = jnp.cumsum(padded)
    pad_start = pad_end - padded
    start = jnp.cumsum(counts) - counts
    pos = inv + jnp.sum(jnp.where(onehot, (pad_start - start)[None, :], 0), axis=1)
    block_start = jnp.arange(nb, dtype=jnp.int32) * bm
    block_e = jnp.minimum(jnp.sum(pad_end[None, :] <= block_start[:, None], axis=1),
                          N_EXPERTS - 1).astype(jnp.int32)
    r = (block_start - _lookup(pad_start, block_e))[:, None] + jnp.arange(bm, dtype=jnp.int32)[None, :]
    src = jnp.clip(_lookup(start, block_e)[:, None] + r, 0, A - 1)
    buf_tok = jnp.where(r < _lookup(counts, block_e)[:, None], order[src] // TOP_K, 0)
    nact = (pad_end[-1:] // bm).astype(jnp.int32)
    return buf_tok.reshape(nb * bm), pos.astype(jnp.int32), block_e, nact


def kernel(x, c, w_ada, b_ada, norm1_g, w_in, pool_w, pool_scale, conv_w, a_log, dt_bias,
           o_norm_g, w_out, norm2_g, w_router_group, b_router_group, w_router_expert,
           b_router_expert, w_gate, w_up, w_down, norm_f_g):
    B, S, D = x.shape
    T = B * S
    depth = w_ada.shape[0]
    pool_width = pool_w.shape[1] * pool_w.shape[2]
    n_main = pool_width + 4 * DN_HEADS * DN_HEAD_DIM
    n_route = N_GROUPS + N_EXPERTS

    assert depth == 1, "kernel supports the single-layer configuration only"
    l = 0
    xt = x.reshape(T, D)
    mod3 = _ada(c, w_ada[l], b_ada[l]).reshape(B, 6, D)

    w_all = w_in[l].astype(BF16)
    w_ba = jnp.pad(w_in[l, :, n_main:], ((0, 0), (0, LANES - 2 * DN_HEADS))).astype(BF16)
    gate_pad = (DN_HEADS, LANES - 2 * DN_HEADS)
    alog_lanes = jnp.pad(a_log[l], gate_pad).reshape(1, LANES)
    dtb_lanes = jnp.pad(dt_bias[l], gate_pad).reshape(1, LANES)
    proj, gates, y_pool = _inproj(xt, mod3, norm1_g[l], w_all, n_main, w_ba, alog_lanes, dtb_lanes,
                                  pool_w[l], pool_scale[l], S)
    y_dn = _deltanet(proj, gates, conv_w[l], o_norm_g[l], B, S, 0)

    w_router = jnp.pad(jnp.concatenate([w_router_group[l], w_router_expert[l]], axis=1),
                       ((0, 0), (0, LANES - n_route)))
    b_router = jnp.pad(jnp.concatenate([b_router_group[l], b_router_expert[l]]),
                       (0, LANES - n_route)).reshape(1, LANES)
    w_router_hi = w_router.astype(BF16)
    w_router_lo = (w_router - w_router_hi.astype(F32)).astype(BF16)
    x2, h2, route = _outproj(y_pool, y_dn, w_out[l].astype(BF16), xt, mod3, norm2_g[l],
                             jnp.concatenate([w_router_hi, w_router_lo], axis=1), b_router, S)

    buf_tok, pos, block_e, nact = _dispatch_plan(route, T)
    ys = _experts(h2, block_e, nact, buf_tok, w_gate[l], w_up[l], w_down[l])
    out = _combine(ys, pos, x2, mod3, route, norm_f_g, S)
    return out.reshape(B, S, D)
```

```python
import functools

import jax
import jax.numpy as jnp
from jax import lax
from jax.experimental import pallas as pl
from jax.experimental.pallas import tpu as pltpu

F32 = jnp.float32
BF16 = jnp.bfloat16

POOL_GROUPS = 4
POOL_WINDOWS = (2, 4, 8, 16)
POOL_HALO = 16
DN_HEADS = 8
DN_HEAD_DIM = 128
CONV_WIDTH = 4
DN_CHUNK = 128
N_GROUPS = 4
EXPERTS_PER_GROUP = 8
N_EXPERTS = N_GROUPS * EXPERTS_PER_GROUP
TOP_K = 2
MOE_BLOCK = 256
EPS = 1e-6
LANES = 128
NEG_BIG = -3.0e38
VMEM_LIMIT = 60 * 1024 * 1024


def _silu(x):
    half = 0.5 * x
    return half * (1.0 + jnp.tanh(half))


def _chunk_cumsum(x, chunk):
    rmod = lax.broadcasted_iota(jnp.int32, x.shape, 0) & (chunk - 1)
    k = 1
    while k < chunk:
        x = x + jnp.where(rmod >= k, pltpu.roll(x, k, axis=0), 0.0)
        k *= 2
    return x


def _dot(a, b):
    return jnp.dot(a, b, preferred_element_type=F32)


def _cparams(*sem):
    return pltpu.CompilerParams(dimension_semantics=sem, vmem_limit_bytes=VMEM_LIMIT)


def _ada_kernel(c_ref, w_ref, b_ref, o_ref):
    ca = _silu(c_ref[...])
    nb = ca.shape[0]
    c_hi = ca.astype(BF16).astype(F32)
    hi_lo = jnp.concatenate([c_hi, ca - c_hi], axis=0).astype(BF16)
    w = w_ref[...]
    w_hi = w.astype(BF16)
    w_lo = (w - w_hi.astype(F32)).astype(BF16)
    both = _dot(hi_lo, w_hi)
    o_ref[...] = both[:nb] + (both[nb:] + _dot(c_hi.astype(BF16), w_lo)) + b_ref[...]


def _ada(c, w_ada, b_ada):
    B, D = c.shape
    N = w_ada.shape[1]
    tn = 1024
    return pl.pallas_call(
        _ada_kernel,
        grid=(N // tn,),
        in_specs=[pl.BlockSpec((B, D), lambda j: (0, 0)),
                  pl.BlockSpec((D, tn), lambda j: (0, j)),
                  pl.BlockSpec((1, tn), lambda j: (0, j))],
        out_specs=pl.BlockSpec((B, tn), lambda j: (0, j)),
        out_shape=jax.ShapeDtypeStruct((B, N), F32),
        compiler_params=_cparams("arbitrary"),
        name="ada",
    )(c, w_ada, b_ada.reshape(1, N))


def _pool_mix(u, halo, pos0, w_ref, sc_ref, o_ref, gd):
    rows = u.shape[0]
    tpos = lax.broadcasted_iota(jnp.int32, (rows, 1), 0) + pos0 + 1
    for g in range(POOL_GROUPS):
        win = POOL_WINDOWS[g]
        cols = slice(g * gd, (g + 1) * gd)
        s = jnp.concatenate([halo[:, cols], u[:, cols]], axis=0)
        k = 1
        while k < win:
            s = s + pltpu.roll(s, k, axis=0)
            k *= 2
        cnt = jnp.minimum(tpos, win).astype(F32)
        diff = s[POOL_HALO:, :] / cnt - u[:, cols]
        y = _dot(diff.astype(BF16), w_ref[g].astype(BF16)) * sc_ref[:, cols]
        o_ref[:, cols] = y.astype(o_ref.dtype)


def _inproj_kernel(x_ref, mod_ref, g_ref, w_ref, wba_ref, alog_ref, dtb_ref, pw_ref, psc_ref,
                   proj_ref, gates_ref, ypool_ref, h_scr, u_scr, halo_scr, *, seq, gd):
    i = pl.program_id(0)
    j = pl.program_id(1)
    ni = pl.num_programs(0)
    nj = pl.num_programs(1)
    tm = x_ref.shape[0]
    cur = i % 2

    def prologue():
        x = x_ref[...]
        ms = jnp.mean(x * x, axis=-1, keepdims=True)
        y = x * lax.rsqrt(ms + EPS) * g_ref[...]
        h = (y * (1.0 + mod_ref[0, 1:2, :]) + mod_ref[0, 0:1, :]).astype(BF16)
        ba = _dot(h, wba_ref[...])
        lane = lax.broadcasted_iota(jnp.int32, ba.shape, 1)
        a = ba + dtb_ref[...]
        softplus = jnp.maximum(a, 0.0) + jnp.log1p(jnp.exp(-jnp.abs(a)))
        gc = _chunk_cumsum(-jnp.exp(alog_ref[...]) * softplus, DN_CHUNK)
        gates_ref[...] = jnp.where(lane < DN_HEADS, jax.nn.sigmoid(ba), gc)
        return h

    @pl.when((i == 0) & (j == 0))
    def _():
        h = prologue()
        h_scr[0] = h
        halo_scr[...] = jnp.zeros_like(halo_scr)
        u_scr[...] = _dot(h, w_ref[...])

    @pl.when((i > 0) & (j == 0))
    def _():
        u_scr[...] = _dot(h_scr[cur], w_ref[...])

    @pl.when(j == 1)
    def _():
        proj_ref[...] = _dot(h_scr[cur], w_ref[...])
        pos0 = (i * tm) % seq
        u = u_scr[...]
        halo = jnp.where(pos0 > 0, halo_scr[...], 0.0)
        _pool_mix(u, halo, pos0, pw_ref, psc_ref, ypool_ref, gd)
        halo_scr[...] = u[tm - POOL_HALO:, :]

    @pl.when((j > 1) & (j < nj - 1))
    def _():
        proj_ref[...] = _dot(h_scr[cur], w_ref[...])

    @pl.when((j == nj - 1) & (i + 1 < ni))
    def _():
        proj_ref[...] = _dot(h_scr[cur], w_ref[...])
        h_scr[1 - cur] = prologue()

    @pl.when((j == nj - 1) & (i + 1 == ni))
    def _():
        proj_ref[...] = _dot(h_scr[cur], w_ref[...])


def _inproj(x2d, mod3, norm_g, w_all, n_main, w_ba, alog_lanes, dtb_lanes, pool_w, pool_scale, seq):
    T, D = x2d.shape
    G, gd, _ = pool_w.shape
    tm, tn = 1024, 1024
    ni, nj = T // tm, n_main // tn
    assert tm % DN_CHUNK == 0 and seq % tm == 0 and n_main % tn == 0 and G * gd == tn and nj >= 3

    def ahead(i, j):
        return jnp.minimum(i + (j == nj - 1).astype(jnp.int32), ni - 1)

    const = lambda i, j: (0, 0)
    return pl.pallas_call(
        functools.partial(_inproj_kernel, seq=seq, gd=gd),
        grid=(ni, nj),
        in_specs=[pl.BlockSpec((tm, D), lambda i, j: (ahead(i, j), 0)),
                  pl.BlockSpec((1, 6, D), lambda i, j: (ahead(i, j) * tm // seq, 0, 0)),
                  pl.BlockSpec((1, D), const),
                  pl.BlockSpec((D, tn), lambda i, j: (0, j)),
                  pl.BlockSpec((D, LANES), const),
                  pl.BlockSpec((1, LANES), const),
                  pl.BlockSpec((1, LANES), const),
                  pl.BlockSpec((G, gd, gd), lambda i, j: (0, 0, 0)),
                  pl.BlockSpec((1, tn), const)],
        out_specs=[pl.BlockSpec((tm, tn), lambda i, j: (i, jnp.maximum(j - 1, 0))),
                   pl.BlockSpec((tm, LANES), lambda i, j: (ahead(i, j), 0)),
                   pl.BlockSpec((tm, tn), lambda i, j: (i, 0))],
        out_shape=[jax.ShapeDtypeStruct((T, n_main - tn), F32),
                   jax.ShapeDtypeStruct((T, LANES), F32),
                   jax.ShapeDtypeStruct((T, tn), BF16)],
        scratch_shapes=[pltpu.VMEM((2, tm, D), BF16), pltpu.VMEM((tm, tn), F32),
                        pltpu.VMEM((POOL_HALO, tn), F32)],
        compiler_params=_cparams("arbitrary", "arbitrary"),
        name="inproj",
    )(x2d, mod3, norm_g.reshape(1, D), w_all, w_ba, alog_lanes, dtb_lanes, pool_w,
      pool_scale.reshape(1, tn))


def _bmm(a, b):
    return jnp.einsum('nij,njk->nik', a.astype(BF16), b.astype(BF16),
                      preferred_element_type=F32)


def _bmm_nt(a, b):
    return jnp.einsum('nid,njd->nij', a.astype(BF16), b.astype(BF16),
                      preferred_element_type=F32)


def _dn_prepare(q_raw, k_raw, v_raw, cwq, cwk, cwv, beta_col, gc_col, *, chunk):
    seq, d = q_raw.shape
    n = seq // chunk
    top = lax.broadcasted_iota(jnp.int32, (8, d), 0)

    def shift(a, sh):
        r = pltpu.roll(a, sh, axis=0)
        return jnp.concatenate([jnp.where(top >= sh, r[:8], 0.0), r[8:]], axis=0)

    def conv_silu(x, cw):
        assert CONV_WIDTH == 4
        w0, w1, w2, w3 = (cw[j:j + 1, :] for j in range(CONV_WIDTH))
        x1 = shift(x, 1)
        return _silu(x * w3 + x1 * w2 + shift(x * w1 + x1 * w0, 2))

    def l2n(x):
        return x * lax.rsqrt(jnp.sum(x * x, axis=-1, keepdims=True) + EPS)

    qn = l2n(conv_silu(q_raw, cwq)) * (d ** -0.5)
    kn = l2n(conv_silu(k_raw, cwk))
    v = conv_silu(v_raw, cwv)
    beta = jnp.broadcast_to(beta_col, (seq, d))
    gc = jnp.broadcast_to(gc_col, (seq, d))

    gc3 = gc.reshape(n, chunk, d)
    ii = lax.broadcasted_iota(jnp.int32, (chunk, chunk), 0)
    jj = lax.broadcasted_iota(jnp.int32, (chunk, chunk), 1)
    incl = (ii >= jj)[None]
    strict = (ii > jj)[None]
    diff = gc3 - jnp.swapaxes(gc3, 1, 2)
    decay = jnp.where(incl, jnp.exp(jnp.where(incl, diff, 0.0)), 0.0)

    kb = kn * beta
    kn3 = kn.reshape(n, chunk, d)
    lmat = jnp.where(strict, _bmm_nt(kb.reshape(n, chunk, d), kn3) * decay, 0.0)
    qk = jnp.where(incl, _bmm_nt(qn.reshape(n, chunk, d), kn3) * decay, 0.0)

    def merge_mask(lv):
        same = (ii >> (lv + 1)) == (jj >> (lv + 1))
        return (same & (((ii >> lv) & 1) == 1) & (((jj >> lv) & 1) == 0))[None]

    tinv = (ii == jj).astype(F32)[None] - jnp.where(merge_mask(0), lmat, 0.0)
    for lv in range(1, chunk.bit_length() - 1):
        a21 = jnp.where(merge_mask(lv), lmat, 0.0)
        tinv = tinv - _bmm(tinv, _bmm(a21, tinv))

    egc = jnp.exp(gc)
    rhs = jnp.concatenate([v * beta, kb * egc], axis=-1).reshape(n, chunk, 2 * d)
    uw = _bmm(tinv, rhs)
    glast = gc3[:, chunk - 1:chunk, :]
    kdt = jnp.swapaxes(kn3 * jnp.exp(glast - gc3), 1, 2)
    kuw = _bmm(kdt, uw)
    quw = _bmm(qk, uw)
    qp = (qn * egc).reshape(n, chunk, d) - quw[..., d:]
    return quw[..., :d], qp, kuw[..., :d], kuw[..., d:], jnp.exp(glast)


def _dn_kernel(q_ref, k_ref, v_ref, z_ref, gates_ref, cwq_ref, cwk_ref, cwv_ref, on_ref,
               o_ref, o_scr, *, chunk, heads_per_step):
    d = DN_HEAD_DIM
    seq = q_ref.shape[0]
    n = seq // chunk
    lane = lax.broadcasted_iota(jnp.int32, (seq, LANES), 1)
    gates = gates_ref[...]
    prepared = []
    for i in range(heads_per_step):
        head = pl.program_id(1) * heads_per_step + i
        cols = slice(i * d, (i + 1) * d)
        beta_col = jnp.sum(jnp.where(lane == head, gates, 0.0), axis=-1, keepdims=True)
        gc_col = jnp.sum(jnp.where(lane == head + DN_HEADS, gates, 0.0), axis=-1, keepdims=True)
        prepared.append(_dn_prepare(q_ref[:, cols], k_ref[:, cols], v_ref[:, cols],
                                    cwq_ref[:, cols], cwk_ref[:, cols], cwv_ref[:, cols],
                                    beta_col, gc_col, chunk=chunk))

    states = [jnp.zeros((d, d), F32) for _ in range(heads_per_step)]
    for c in range(n):
        for i, (o0, qp, kub, kuw, eglast) in enumerate(prepared):
            sb = states[i].astype(BF16)
            o_scr[c * chunk:(c + 1) * chunk, i * d:(i + 1) * d] = o0[c] + _dot(qp[c].astype(BF16), sb)
            states[i] = states[i] * eglast[c] + kub[c] - _dot(kuw[c].astype(BF16), sb)

    for i in range(heads_per_step):
        cols = slice(i * d, (i + 1) * d)
        o = o_scr[:, cols]
        o = o * lax.rsqrt(jnp.mean(o * o, axis=-1, keepdims=True) + EPS) * on_ref[...]
        o_ref[:, cols] = (o * _silu(z_ref[:, cols])).astype(o_ref.dtype)


def _deltanet(proj, gates, conv_w, o_norm_g, batch, seq, col0):
    T = proj.shape[0]
    d = DN_HEAD_DIM
    H = DN_HEADS
    hps = 2
    w = hps * d
    cb = col0 // w
    nhb = H // hps

    def colspec(off):
        return pl.BlockSpec((seq, w), lambda b, h: (b, off + h))

    def convspec(off):
        return pl.BlockSpec((CONV_WIDTH, w), lambda b, h: (0, off + h))

    return pl.pallas_call(
        functools.partial(_dn_kernel, chunk=DN_CHUNK, heads_per_step=hps),
        grid=(batch, nhb),
        in_specs=[colspec(cb), colspec(cb + nhb), colspec(cb + 2 * nhb), colspec(cb + 3 * nhb),
                  pl.BlockSpec((seq, LANES), lambda b, h: (b, 0)),
                  convspec(0), convspec(nhb), convspec(2 * nhb),
                  pl.BlockSpec((1, d), lambda b, h: (0, 0))],
        out_specs=pl.BlockSpec((seq, w), lambda b, h: (b, h)),
        out_shape=jax.ShapeDtypeStruct((T, H * d), BF16),
        scratch_shapes=[pltpu.VMEM((seq, w), F32)],
        compiler_params=_cparams("arbitrary", "arbitrary"),
        name="deltanet",
    )(proj, proj, proj, proj, gates, conv_w, conv_w, conv_w, o_norm_g.reshape(1, d))


def _route(logits):
    lane = lax.broadcasted_iota(jnp.int32, logits.shape, 1)
    lanef = lane.astype(F32)
    far = float(LANES)

    def first_max(vals):
        m = jnp.max(vals, axis=-1, keepdims=True)
        idx = jnp.min(jnp.where(vals == m, lanef, far), axis=-1, keepdims=True)
        return m, idx

    gl = jnp.where(lane < N_GROUPS, logits, NEG_BIG)
    gmax, gidx = first_max(gl)
    p_top = 1.0 / jnp.sum(jnp.exp(gl - gmax), axis=-1, keepdims=True)
    lo = N_GROUPS + EXPERTS_PER_GROUP * gidx
    el = jnp.where(lanef >= lo, jnp.where(lanef < lo + EXPERTS_PER_GROUP, logits, NEG_BIG), NEG_BIG)
    m1, i1 = first_max(el)
    m2, i2 = first_max(jnp.where(lanef == i1, NEG_BIG, el))
    t = jnp.exp(m2 - m1)
    w1 = p_top / (1.0 + t)
    w2 = w1 * t
    return jnp.where(lane == 0, i1 - N_GROUPS,
                     jnp.where(lane == 1, i2 - N_GROUPS,
                               jnp.where(lane == 2, w1, jnp.where(lane == 3, w2, 0.0))))


def _slab_store(ref, val):
    rows, width = val.shape
    ns = width // LANES
    for s in range(ns):
        ref[pl.ds(s, rows, stride=ns), :] = val[:, s * LANES:(s + 1) * LANES]


def _slab_load(ref, first_row, rows, ns, pitch):
    return [ref[pl.ds(first_row * pitch + s, rows, stride=pitch), :] for s in range(ns)]


def _gather_pitch(ns):
    return ns + 8 if (ns // 8) % 2 == 0 else ns


def _outproj_kernel(yp_ref, yd_ref, wo_ref, x_ref, mod_ref, g_ref, wr_ref, br_ref,
                    x2_ref, h2_ref, route_ref):
    half = yp_ref.shape[1]
    out = _dot(yp_ref[...], wo_ref[:half, :]) + _dot(yd_ref[...], wo_ref[half:, :])
    x2 = x_ref[...] + mod_ref[0, 2:3, :] * out
    x2_ref[...] = x2
    ms = jnp.mean(x2 * x2, axis=-1, keepdims=True)
    y = x2 * lax.rsqrt(ms + EPS) * g_ref[...]
    h2 = y * (1.0 + mod_ref[0, 4:5, :]) + mod_ref[0, 3:4, :]
    _slab_store(h2_ref, h2)
    hi = h2.astype(BF16)
    lo = (h2 - hi.astype(F32)).astype(BF16)
    both = _dot(hi, wr_ref[...])
    logits = both[:, :LANES] + (_dot(lo, wr_ref[:, :LANES]) + both[:, LANES:]) + br_ref[...]
    route_ref[...] = _route(logits)


def _outproj(y_pool, y_dn, w_out_bf, x2d, mod3, norm_g, w_router, b_router, seq):
    T, D = x2d.shape
    half = y_pool.shape[1]
    tm = 512
    row = lambda i: (i, 0)
    const = lambda i: (0, 0)
    return pl.pallas_call(
        _outproj_kernel,
        grid=(T // tm,),
        in_specs=[pl.BlockSpec((tm, half), row),
                  pl.BlockSpec((tm, half), row),
                  pl.BlockSpec((2 * half, D), const, pipeline_mode=pl.Buffered(1)),
                  pl.BlockSpec((tm, D), row),
                  pl.BlockSpec((1, 6, D), lambda i: (i * tm // seq, 0, 0)),
                  pl.BlockSpec((1, D), const),
                  pl.BlockSpec((D, 2 * LANES), const),
                  pl.BlockSpec((1, LANES), const)],
        out_specs=[pl.BlockSpec((tm, D), row),
                   pl.BlockSpec((tm * (D // LANES), LANES), row),
                   pl.BlockSpec((tm, LANES), row)],
        out_shape=[jax.ShapeDtypeStruct((T, D), F32),
                   jax.ShapeDtypeStruct((T * (D // LANES), LANES), F32),
                   jax.ShapeDtypeStruct((T, LANES), F32)],
        compiler_params=_cparams("arbitrary"),
        name="outproj",
    )(y_pool, y_dn, w_out_bf, x2d, mod3, norm_g.reshape(1, D), w_router, b_router)


GATHER_UNROLL = 8


def _row_gather(idx_ref, nrows, ns, pitch, src_hbm, dst, sem, queues=1):
    def body(k, carry):
        for q in range(queues):
            r = k * queues + q
            src_row = pl.multiple_of(idx_ref[0, 0, r] * ns, ns)
            dst_row = pl.multiple_of(r * pitch, 8)
            pltpu.make_async_copy(src_hbm.at[pl.ds(src_row, ns), :],
                                  dst.at[pl.ds(dst_row, ns), :], sem).start(priority=q)
        return carry
    lax.fori_loop(0, nrows // queues, body, 0, unroll=GATHER_UNROLL // queues)


def _row_gather_wait(nrows, ns, src_hbm, dst, sem):
    pltpu.make_async_copy(src_hbm.at[pl.ds(0, nrows * ns), :], dst.at[pl.ds(0, nrows * ns), :],
                          sem).wait()


WEIGHT_DMA_PRIORITY = 1


def _expert_kernel(be_ref, first_ref, wslot_ref, nexte_ref, nact_ref, tok_ref, tokn_ref,
                   h_hbm, wg_hbm, wu_hbm, wd_hbm, ys_ref, xbuf, wg_buf, wu_buf, wd_buf,
                   sem, wsem, *, bm, ns, pitch):
    i = pl.program_id(0)
    nact = nact_ref[0]
    slot = i % 2
    wslot = wslot_ref[i]

    def weight_copies(e, s):
        return (pltpu.make_async_copy(wg_hbm.at[e], wg_buf.at[s], wsem.at[s]),
                pltpu.make_async_copy(wu_hbm.at[e], wu_buf.at[s], wsem.at[s]),
                pltpu.make_async_copy(wd_hbm.at[e], wd_buf.at[s], wsem.at[s]))

    @pl.when(i == 0)
    def _():
        for cp in weight_copies(be_ref[0], 0):
            cp.start(priority=WEIGHT_DMA_PRIORITY)
        _row_gather(tok_ref, bm, ns, pitch, h_hbm, xbuf.at[0], sem.at[0])

    @pl.when((first_ref[i] == 1) & (nexte_ref[i] >= 0))
    def _():
        for cp in weight_copies(nexte_ref[i], 1 - wslot):
            cp.start(priority=WEIGHT_DMA_PRIORITY)

    @pl.when(i + 1 < nact)
    def _():
        _row_gather(tokn_ref, bm, ns, pitch, h_hbm, xbuf.at[1 - slot], sem.at[1 - slot])

    @pl.when(first_ref[i] == 1)
    def _():
        for cp in weight_copies(be_ref[i], wslot):
            cp.wait()

    @pl.when(i < nact)
    def _():
        _row_gather_wait(bm, ns, h_hbm, xbuf.at[slot], sem.at[slot])
        xb = jnp.concatenate(_slab_load(xbuf.at[slot], 0, bm, ns, pitch), axis=-1).astype(BF16)
        gate = _dot(xb, wg_buf[wslot].astype(BF16))
        up = _dot(xb, wu_buf[wslot].astype(BF16))
        hid = (_silu(gate) * up).astype(BF16)
        _slab_store(ys_ref, _dot(hid, wd_buf[wslot].astype(BF16)))

    @pl.when(i >= nact)
    def _():
        ys_ref[...] = jnp.zeros_like(ys_ref)


def _experts(h2_slab, block_e, nact, buf_tok, w_gate, w_up, w_down):
    E, D, De = w_gate.shape
    ns = D // LANES
    pitch = _gather_pitch(ns)
    bm = MOE_BLOCK
    n_pad = buf_tok.shape[0]
    nb = n_pad // bm
    tok3 = buf_tok.reshape(nb, 1, bm)

    idx = jnp.arange(nb, dtype=jnp.int32)
    active = idx < nact[0]
    prev_e = jnp.concatenate([jnp.full((1,), -1, jnp.int32), block_e[:-1]])
    first = (active & (block_e != prev_e)).astype(jnp.int32)
    wslot = ((jnp.cumsum(first) - 1) % 2).astype(jnp.int32)
    later_first = (first[None, :] == 1) & (idx[None, :] > idx[:, None])
    nxt = jnp.min(jnp.where(later_first, idx[None, :], nb), axis=1)
    next_e = jnp.where(nxt < nb, _lookup(block_e, jnp.minimum(nxt, nb - 1)), -1).astype(jnp.int32)

    any_spec = pl.BlockSpec(memory_space=pl.ANY)
    grid_spec = pltpu.PrefetchScalarGridSpec(
        num_scalar_prefetch=5,
        grid=(nb,),
        in_specs=[pl.BlockSpec((1, 1, bm), lambda i, *_: (i, 0, 0), memory_space=pltpu.SMEM),
                  pl.BlockSpec((1, 1, bm), lambda i, *_: (jnp.minimum(i + 1, nb - 1), 0, 0),
                               memory_space=pltpu.SMEM),
                  any_spec, any_spec, any_spec, any_spec],
        out_specs=pl.BlockSpec((bm * ns, LANES), lambda i, *_: (i, 0)),
        scratch_shapes=[pltpu.VMEM((2, bm * pitch, LANES), F32),
                        pltpu.VMEM((2, D, De), F32), pltpu.VMEM((2, D, De), F32),
                        pltpu.VMEM((2, De, D), F32),
                        pltpu.SemaphoreType.DMA((2,)), pltpu.SemaphoreType.DMA((2,))],
    )
    return pl.pallas_call(
        functools.partial(_expert_kernel, bm=bm, ns=ns, pitch=pitch),
        grid_spec=grid_spec,
        out_shape=jax.ShapeDtypeStruct((n_pad * ns, LANES), F32),
        compiler_params=_cparams("arbitrary"),
        name="experts",
    )(block_e, first, wslot, next_e, nact, tok3, tok3, h2_slab, w_gate, w_up, w_down)


def _combine_kernel(pos_ref, posn_ref, ys_hbm, x2_ref, mod_ref, route_ref, g_ref, o_ref,
                    ybuf, sem, *, tm, ns, pitch):
    i = pl.program_id(0)
    nsteps = pl.num_programs(0)
    slot = i % 2
    nrows = TOP_K * tm

    @pl.when(i == 0)
    def _():
        _row_gather(pos_ref, nrows, ns, pitch, ys_hbm, ybuf.at[0], sem.at[0], queues=2)

    @pl.when(i + 1 < nsteps)
    def _():
        _row_gather(posn_ref, nrows, ns, pitch, ys_hbm, ybuf.at[1 - slot], sem.at[1 - slot], queues=2)

    _row_gather_wait(nrows, ns, ys_hbm, ybuf.at[slot], sem.at[slot])
    route = route_ref[...]
    w0 = route[:, 2:3]
    w1 = route[:, 3:4]
    y0 = _slab_load(ybuf.at[slot], 0, tm, ns, pitch)
    y1 = _slab_load(ybuf.at[slot], tm, tm, ns, pitch)
    y = jnp.concatenate([w0 * a + w1 * b for a, b in zip(y0, y1)], axis=-1)
    x3 = x2_ref[...] + mod_ref[0, 5:6, :] * y
    ms = jnp.mean(x3 * x3, axis=-1, keepdims=True)
    o_ref[...] = x3 * lax.rsqrt(ms + EPS) * g_ref[...]


def _combine(ys_slab, pos, x2, mod3, route, norm_g, seq):
    T, D = x2.shape
    ns = D // LANES
    pitch = _gather_pitch(ns)
    tm = 256
    nt = T // tm
    pos3 = pos.reshape(nt, tm, TOP_K).transpose(0, 2, 1).reshape(nt, 1, TOP_K * tm)
    row = lambda i: (i, 0)
    return pl.pallas_call(
        functools.partial(_combine_kernel, tm=tm, ns=ns, pitch=pitch),
        grid=(nt,),
        in_specs=[pl.BlockSpec((1, 1, TOP_K * tm), lambda i: (i, 0, 0), memory_space=pltpu.SMEM),
                  pl.BlockSpec((1, 1, TOP_K * tm), lambda i: (jnp.minimum(i + 1, nt - 1), 0, 0),
                               memory_space=pltpu.SMEM),
                  pl.BlockSpec(memory_space=pl.ANY),
                  pl.BlockSpec((tm, D), row),
                  pl.BlockSpec((1, 6, D), lambda i: (i * tm // seq, 0, 0)),
                  pl.BlockSpec((tm, LANES), row),
                  pl.BlockSpec((1, D), lambda i: (0, 0))],
        out_specs=pl.BlockSpec((tm, D), row),
        out_shape=jax.ShapeDtypeStruct((T, D), F32),
        scratch_shapes=[pltpu.VMEM((2, TOP_K * tm * pitch, LANES), F32),
                        pltpu.SemaphoreType.DMA((2,))],
        compiler_params=_cparams("arbitrary"),
        name="combine",
    )(pos3, pos3, ys_slab, x2, mod3, route, norm_g.reshape(1, D))


def _lookup(table, idx):
    hit = idx[..., None] == jnp.arange(table.shape[0], dtype=idx.dtype)
    return jnp.sum(jnp.where(hit, table, 0), axis=-1)


def _dispatch_plan(route, n_tokens):
    A = n_tokens * TOP_K
    bm = MOE_BLOCK
    nb = (A + N_EXPERTS * (bm - 1)) // bm + 1
    flat_e = route[:, :TOP_K].astype(jnp.int32).reshape(A)
    iota = jnp.arange(A, dtype=jnp.int32)
    _, order = lax.sort_key_val(flat_e, iota)
    _, inv = lax.sort_key_val(order, iota)
    onehot = flat_e[:, None] == jnp.arange(N_EXPERTS, dtype=jnp.int32)[None, :]
    counts = lax.optimization_barrier(jnp.sum(onehot, axis=0, dtype=jnp.int32))
    padded = (counts + bm - 1) // bm * bm
    pad_end = jnp.cumsum(padded)
    pad_start = pad_end - padded
    start = jnp.cumsum(counts) - counts
    pos = inv + jnp.sum(jnp.where(onehot, (pad_start - start)[None, :], 0), axis=1)
    block_start = jnp.arange(nb, dtype=jnp.int32) * bm
    block_e = jnp.minimum(jnp.sum(pad_end[None, :] <= block_start[:, None], axis=1),
                          N_EXPERTS - 1).astype(jnp.int32)
    r0 = block_start - _lookup(pad_start, block_e)
    r = r0[:, None] + jnp.arange(bm, dtype=jnp.int32)[None, :]
    tok_sorted = jnp.concatenate([order // TOP_K, jnp.zeros((bm,), jnp.int32)])
    first = jnp.clip(_lookup(start, block_e) + r0, 0, A)
    window = jax.vmap(lambda s: lax.dynamic_slice(tok_sorted, (s,), (bm,)))(first)
    buf_tok = jnp.where(r < _lookup(counts, block_e)[:, None], window, 0)
    nact = (pad_end[-1:] // bm).astype(jnp.int32)
    return buf_tok.reshape(nb * bm), pos.astype(jnp.int32), block_e, nact


def kernel(x, c, w_ada, b_ada, norm1_g, w_in, pool_w, pool_scale, conv_w, a_log, dt_bias,
           o_norm_g, w_out, norm2_g, w_router_group, b_router_group, w_router_expert,
           b_router_expert, w_gate, w_up, w_down, norm_f_g):
    B, S, D = x.shape
    T = B * S
    depth = w_ada.shape[0]
    pool_width = pool_w.shape[1] * pool_w.shape[2]
    n_main = pool_width + 4 * DN_HEADS * DN_HEAD_DIM
    n_route = N_GROUPS + N_EXPERTS

    assert depth == 1, "kernel supports the single-layer configuration only"
    l = 0
    xt = x.reshape(T, D)
    mod3 = _ada(c, w_ada[l], b_ada[l]).reshape(B, 6, D)

    w_all = w_in[l].astype(BF16)
    w_ba = jnp.pad(w_in[l, :, n_main:], ((0, 0), (0, LANES - 2 * DN_HEADS))).astype(BF16)
    gate_pad = (DN_HEADS, LANES - 2 * DN_HEADS)
    alog_lanes = jnp.pad(a_log[l], gate_pad).reshape(1, LANES)
    dtb_lanes = jnp.pad(dt_bias[l], gate_pad).reshape(1, LANES)
    proj, gates, y_pool = _inproj(xt, mod3, norm1_g[l], w_all, n_main, w_ba, alog_lanes, dtb_lanes,
                                  pool_w[l], pool_scale[l], S)
    y_dn = _deltanet(proj, gates, conv_w[l], o_norm_g[l], B, S, 0)

    w_router = jnp.pad(jnp.concatenate([w_router_group[l], w_router_expert[l]], axis=1),
                       ((0, 0), (0, LANES - n_route)))
    b_router = jnp.pad(jnp.concatenate([b_router_group[l], b_router_expert[l]]),
                       (0, LANES - n_route)).reshape(1, LANES)
    w_router_hi = w_router.astype(BF16)
    w_router_lo = (w_router - w_router_hi.astype(F32)).astype(BF16)
    x2, h2, route = _outproj(y_pool, y_dn, w_out[l].astype(BF16), xt, mod3, norm2_g[l],
                             jnp.concatenate([w_router_hi, w_router_lo], axis=1), b_router, S)

    buf_tok, pos, block_e, nact = _dispatch_plan(route, T)
    ys = _experts(h2, block_e, nact, buf_tok, w_gate[l], w_up[l], w_down[l])
    out = _combine(ys, pos, x2, mod3, route, norm_f_g, S)
    return out.reshape(B, S, D)
```
